```python
import jax, jax.numpy as jnp
from jax import lax
import numpy as np

D_MODEL = 4096
BATCH = 4
SEQ = 4096
DEPTH = 1

HEAD_DIM = 128
N_Q_HEADS = D_MODEL // 256
N_KV_HEADS = N_Q_HEADS // 4
Q_PER_KV = N_Q_HEADS // N_KV_HEADS
ATTN_WIDTH = N_Q_HEADS * HEAD_DIM
KV_WIDTH = N_KV_HEADS * HEAD_DIM
WINDOW = 128
BLOCK = 128
ROT_DIM = HEAD_DIM // 4
ROPE_THETA = 500000.0

SGU_CHUNK = 128
SGU_GROUPS = D_MODEL // 256
SGU_GROUP_WIDTH = 128
SGU_WIDTH = SGU_GROUPS * SGU_GROUP_WIDTH

FFN_HIDDEN = -(-(8 * D_MODEL) // (3 * 256)) * 256

IN_WIDTH = ATTN_WIDTH + 2 * KV_WIDTH + 2 * SGU_WIDTH + 2 * D_MODEL
N_MOD = 6
RMS_EPS = 1e-6
LN_EPS = 1e-5

kernel_name = "hybrid_sgu_swa_sink_adaln_block"


def _rms_norm(x, g):
    xf = x.astype(jnp.float32)
    y = xf * lax.rsqrt(jnp.mean(xf * xf, axis=-1, keepdims=True) + RMS_EPS)
    return (y * g.astype(jnp.float32)).astype(x.dtype)


def _layer_norm(x, g, b):
    xf = x.astype(jnp.float32)
    mu = jnp.mean(xf, axis=-1, keepdims=True)
    xc = xf - mu
    y = xc * lax.rsqrt(jnp.mean(xc * xc, axis=-1, keepdims=True) + LN_EPS)
    return (y * g.astype(jnp.float32) + b.astype(jnp.float32)).astype(x.dtype)


def _modulate(h, shift, scale):
    return h * (1 + scale[:, None, :]) + shift[:, None, :]


def _rope_tables(positions, dtype):
    inv_freq = ROPE_THETA ** (-jnp.arange(0, ROT_DIM, 2, dtype=jnp.float32) / ROT_DIM)
    ang = positions.astype(jnp.float32)[..., None] * inv_freq
    return jnp.cos(ang)[:, :, None, :].astype(dtype), jnp.sin(ang)[:, :, None, :].astype(dtype)


def _partial_rope(t, cos, sin):
    half = ROT_DIM // 2
    x1, x2, rest = t[..., :half], t[..., half:ROT_DIM], t[..., ROT_DIM:]
    return jnp.concatenate([x1 * cos - x2 * sin, x2 * cos + x1 * sin, rest], axis=-1)


def _sliding_window_attention(q, k, v, sinks):
    B, S = q.shape[0], q.shape[1]
    nb = S // BLOCK
    qb = q.reshape(B, nb, BLOCK, N_KV_HEADS, Q_PER_KV, HEAD_DIM)

    def with_prev(t):
        tb = t.reshape(B, nb, BLOCK, N_KV_HEADS, HEAD_DIM)
        prev = jnp.pad(tb[:, :-1], ((0, 0), (1, 0), (0, 0), (0, 0), (0, 0)))
        return jnp.concatenate([prev, tb], axis=2)

    kk, vv = with_prev(k), with_prev(v)
    s = jnp.einsum('bnqhgd,bnkhd->bnhgqk', qb, kk,
                   preferred_element_type=jnp.float32) * (HEAD_DIM ** -0.5)
    blk = jnp.arange(nb)[:, None, None]
    qi = jnp.arange(BLOCK)[None, :, None]
    ki = jnp.arange(2 * BLOCK)[None, None, :]
    diff = qi + BLOCK - ki
    kpos = (blk - 1) * BLOCK + ki
    valid = (diff >= 0) & (diff < WINDOW) & (kpos >= 0)
    s = jnp.where(valid[:, None, None, :, :], s, -jnp.inf)
    sink = sinks.astype(jnp.float32).reshape(1, 1, N_KV_HEADS, Q_PER_KV, 1, 1)
    m = jnp.maximum(jnp.max(s, axis=-1, keepdims=True), sink)
    p = jnp.exp(s - m)
    denom = jnp.sum(p, axis=-1, keepdims=True) + jnp.exp(sink - m)
    o = jnp.einsum('bnhgqk,bnkhd->bnqhgd', (p / denom).astype(vv.dtype), vv)
    return o.reshape(B, S, ATTN_WIDTH)


def _spatial_gating(u, v, ln_g, ln_b, w_s, b_s):
    B, S = u.shape[0], u.shape[1]
    nc = S // SGU_CHUNK
    vn = _layer_norm(v, ln_g, ln_b).reshape(B, nc, SGU_CHUNK, SGU_GROUPS, SGU_GROUP_WIDTH)
    causal = jnp.tril(jnp.ones((SGU_CHUNK, SGU_CHUNK), dtype=bool))
    w = jnp.where(causal[None], w_s, jnp.zeros_like(w_s))
    mixed = jnp.einsum('gts,bnsgc->bntgc', w, vn) + b_s.T[:, :, None]
    return u * mixed.reshape(B, S, SGU_WIDTH)


def setup_inputs(seed: int = 0) -> dict:
    key = jax.random.key(seed)
    ks = jax.random.split(key, 24)
    f32 = jnp.float32

    def w(k, shape, fan_in, mult=1.0):
        return jax.random.normal(k, shape, f32) * (mult * fan_in ** -0.5)

    def gain(k, shape):
        return 1.0 + 0.02 * jax.random.normal(k, shape, f32)

    L, D = DEPTH, D_MODEL
    return {
        "x": jax.random.normal(ks[0], (BATCH, SEQ, D), f32),
        "c": jax.random.normal(ks[1], (BATCH, D), f32),
        "positions": jnp.broadcast_to(jnp.arange(SEQ, dtype=jnp.int32)[None, :], (BATCH, SEQ)),
        "w_ada": w(ks[2], (L, D, N_MOD * D), D, 0.5),
        "b_ada": 0.02 * jax.random.normal(ks[3], (L, N_MOD * D), f32),
        "g_pre_mix": gain(ks[4], (L, D)),
        "w_in": w(ks[5], (L, D, IN_WIDTH), D),
        "attn_sinks": 0.5 * jax.random.normal(ks[6], (L, N_Q_HEADS), f32),
        "sgu_ln_g": gain(ks[7], (L, SGU_WIDTH)),
        "sgu_ln_b": 0.02 * jax.random.normal(ks[8], (L, SGU_WIDTH), f32),
        "sgu_w": w(ks[9], (L, SGU_GROUPS, SGU_CHUNK, SGU_CHUNK), SGU_CHUNK),
        "sgu_b": 1.0 + 0.02 * jax.random.normal(ks[10], (L, SGU_GROUPS, SGU_CHUNK), f32),
        "w_proj_sgu": w(ks[11], (L, SGU_WIDTH, D), SGU_WIDTH),
        "w_proj_attn": w(ks[12], (L, ATTN_WIDTH, D), ATTN_WIDTH),
        "w_out": w(ks[13], (L, D, D), D),
        "g_post_mix": gain(ks[14], (L, D)),
        "g_pre_ffn": gain(ks[15], (L, D)),
        "w_gate": w(ks[16], (L, D, FFN_HIDDEN), D),
        "w_up": w(ks[17], (L, D, FFN_HIDDEN), D),
        "w_down": w(ks[18], (L, FFN_HIDDEN, D), FFN_HIDDEN),
        "g_post_ffn": gain(ks[19], (L, D)),
    }


def reference(x, c, positions, w_ada, b_ada, g_pre_mix, w_in, attn_sinks, sgu_ln_g, sgu_ln_b,
              sgu_w, sgu_b, w_proj_sgu, w_proj_attn, w_out, g_post_mix, g_pre_ffn,
              w_gate, w_up, w_down, g_post_ffn):
    B, S = x.shape[0], x.shape[1]
    cos, sin = _rope_tables(positions, x.dtype)
    c_act = jax.nn.silu(c)
    o1 = ATTN_WIDTH
    o2 = o1 + KV_WIDTH
    o3 = o2 + KV_WIDTH
    o4 = o3 + SGU_WIDTH
    o5 = o4 + SGU_WIDTH
    o6 = o5 + D_MODEL
    for l in range(DEPTH):
        mod = c_act @ w_ada[l] + b_ada[l]
        sh1, sc1, gt1, sh2, sc2, gt2 = jnp.split(mod, N_MOD, axis=-1)

        h = _modulate(_rms_norm(x, g_pre_mix[l]), sh1, sc1)
        z = h @ w_in[l]
        q, k, v, su, sv, ga, gb = jnp.split(z, [o1, o2, o3, o4, o5, o6], axis=-1)

        q = _partial_rope(q.reshape(B, S, N_Q_HEADS, HEAD_DIM), cos, sin)
        k = _partial_rope(k.reshape(B, S, N_KV_HEADS, HEAD_DIM), cos, sin)
        v = v.reshape(B, S, N_KV_HEADS, HEAD_DIM)
        attn_out = _sliding_window_attention(q, k, v, attn_sinks[l]) @ w_proj_attn[l]

        su = jax.nn.gelu(su, approximate=False)
        sv = jax.nn.gelu(sv, approximate=False)
        sgu_out = _spatial_gating(su, sv, sgu_ln_g[l], sgu_ln_b[l], sgu_w[l], sgu_b[l]) @ w_proj_sgu[l]

        merged = jax.nn.sigmoid(ga) * sgu_out + jax.nn.sigmoid(gb) * attn_out
        y = _rms_norm(merged @ w_out[l], g_post_mix[l])
        x = x + gt1[:, None, :] * y

        h = _modulate(_rms_norm(x, g_pre_ffn[l]), sh2, sc2)
        f = (jax.nn.silu(h @ w_gate[l]) * (h @ w_up[l])) @ w_down[l]
        x = x + gt2[:, None, :] * _rms_norm(f, g_post_ffn[l])
    return x
```

```python
import functools

import jax
import jax.numpy as jnp
from jax import lax
from jax.experimental import pallas as pl
from jax.experimental.pallas import tpu as pltpu

D_MODEL = 4096
HEAD_DIM = 128
N_Q_HEADS = 16
N_KV_HEADS = 4
Q_PER_KV = 4
ATTN_WIDTH = N_Q_HEADS * HEAD_DIM
KV_WIDTH = N_KV_HEADS * HEAD_DIM
QKV_WIDTH = ATTN_WIDTH + 2 * KV_WIDTH
WINDOW = 128
BLOCK = 128
ROT_DIM = 32
ROPE_THETA = 500000.0
SGU_CHUNK = 128
SGU_GROUPS = 16
SGU_WIDTH = 2048
FFN_HIDDEN = 11008
N_MOD = 6
RMS_EPS = 1e-6
LN_EPS = 1e-5

VMEM_LIMIT_BYTES = 56 * 1024 * 1024

BF16 = jnp.bfloat16
F32 = jnp.float32


def _params(*semantics):
    return pltpu.CompilerParams(dimension_semantics=semantics,
                                vmem_limit_bytes=VMEM_LIMIT_BYTES)


def _rms(v, g):
    return v * lax.rsqrt(jnp.mean(v * v, axis=-1, keepdims=True) + RMS_EPS) * g


ADA_BN = 1024


def _ada_kernel(c_ref, w_ref, b_ref, o_ref):
    c_act = jax.nn.silu(c_ref[...]).astype(BF16)
    o_ref[...] = jnp.dot(c_act, w_ref[...].astype(BF16),
                         preferred_element_type=F32) + b_ref[...]


def _ada_mod(c_pad, w_ada, b_ada):
    rows, d = c_pad.shape
    n = w_ada.shape[1]
    return pl.pallas_call(
        _ada_kernel,
        grid=(n // ADA_BN,),
        in_specs=[pl.BlockSpec((rows, d), lambda j: (0, 0)),
                  pl.BlockSpec((d, ADA_BN), lambda j: (0, j)),
                  pl.BlockSpec((1, ADA_BN), lambda j: (0, j))],
        out_specs=pl.BlockSpec((rows, ADA_BN), lambda j: (0, j)),
        out_shape=jax.ShapeDtypeStruct((rows, n), F32),
        compiler_params=_params("arbitrary"),
        name="ada_mod",
    )(c_pad, w_ada, b_ada.reshape(1, n))


NORM_BM = 512


def _prenorm_kernel(x_ref, g_ref, sh_ref, sc_ref, o_ref):
    y = _rms(x_ref[...], g_ref[...])
    o_ref[...] = (y * (1.0 + sc_ref[...]) + sh_ref[...]).astype(o_ref.dtype)


def _mod_spec(blocks_per_batch, m):
    return pl.BlockSpec((None, 1, D_MODEL),
                        lambda i: ((i // blocks_per_batch) * N_MOD + m, 0, 0))


def _prenorm(x2, g, mod3, seq, m_shift, m_scale):
    t, d = x2.shape
    bpb = seq // NORM_BM
    return pl.pallas_call(
        _prenorm_kernel,
        grid=(t // NORM_BM,),
        in_specs=[pl.BlockSpec((NORM_BM, d), lambda i: (i, 0)),
                  pl.BlockSpec((1, d), lambda i: (0, 0)),
                  _mod_spec(bpb, m_shift), _mod_spec(bpb, m_scale)],
        out_specs=pl.BlockSpec((NORM_BM, d), lambda i: (i, 0)),
        out_shape=jax.ShapeDtypeStruct((t, d), BF16),
        compiler_params=_params("arbitrary"),
        name="prenorm",
    )(x2, g.reshape(1, d), mod3, mod3)


PROJ_BM = 1024
PROJ_BN = 512


def _cast_weight_once(w_ref, wbf_ref):
    @pl.when(pl.program_id(1) == 0)
    def _():
        wbf_ref[...] = w_ref[...].astype(BF16)


def _rope(z, c_ref, s1_ref, s2_ref):
    reps = z.shape[1] // HEAD_DIM
    c = jnp.tile(c_ref[...], (1, reps))
    s1 = jnp.tile(s1_ref[...], (1, reps))
    s2 = jnp.tile(s2_ref[...], (1, reps))
    half = ROT_DIM // 2
    up = pltpu.roll(z, z.shape[1] - half, 1)
    down = pltpu.roll(z, half, 1)
    return z * c + up * s1 + down * s2


def _qkv_kernel(h_ref, w_ref, c_ref, s1_ref, s2_ref, o_ref, wbf_ref):
    _cast_weight_once(w_ref, wbf_ref)
    z = jnp.dot(h_ref[...], wbf_ref[...], preferred_element_type=F32)
    is_v = pl.program_id(0) >= (ATTN_WIDTH + KV_WIDTH) // PROJ_BN

    @pl.when(jnp.logical_not(is_v))
    def _():
        o_ref[...] = _rope(z, c_ref, s1_ref, s2_ref).astype(o_ref.dtype)

    @pl.when(is_v)
    def _():
        o_ref[...] = z.astype(o_ref.dtype)


def _act_kernel(act, h_ref, w_ref, o_ref, wbf_ref):
    _cast_weight_once(w_ref, wbf_ref)
    z = jnp.dot(h_ref[...], wbf_ref[...], preferred_element_type=F32)
    o_ref[...] = act(z).astype(o_ref.dtype)


def _gelu(z):
    return 0.5 * z * (1.0 + lax.erf(z * (0.5 ** 0.5)))


def _in_proj(kernel_fn, h, w_in, col_off, width, extra=(), name=None):
    t, d = h.shape
    off_blocks = col_off // PROJ_BN
    extra_specs = [pl.BlockSpec((PROJ_BM, a.shape[1]), lambda j, i: (i, 0)) for a in extra]
    return pl.pallas_call(
        kernel_fn,
        grid=(width // PROJ_BN, t // PROJ_BM),
        in_specs=[pl.BlockSpec((PROJ_BM, d), lambda j, i: (i, 0)),
                  pl.BlockSpec((d, PROJ_BN), lambda j, i: (0, j + off_blocks))] + extra_specs,
        out_specs=pl.BlockSpec((PROJ_BM, PROJ_BN), lambda j, i: (i, j)),
        out_shape=jax.ShapeDtypeStruct((t, width), BF16),
        scratch_shapes=[pltpu.VMEM((d, PROJ_BN), BF16)],
        compiler_params=_params("arbitrary", "arbitrary"),
        name=name,
    )(h, w_in, *extra)


ATT_ROWS = 1024


def _attn_kernel(sink_ref, q_ref, kc_ref, kp_ref, vc_ref, vp_ref, o_ref, *, blocks_per_seq):
    r = pl.program_id(0)
    h = pl.program_id(1)
    seq_start = (r % blocks_per_seq) == 0
    rows = Q_PER_KV * BLOCK
    qi = lax.broadcasted_iota(jnp.int32, (rows, 2 * BLOCK), 0) % BLOCK
    ki = lax.broadcasted_iota(jnp.int32, (rows, 2 * BLOCK), 1)
    diff = qi + BLOCK - ki
    band = (diff >= 0) & (diff < WINDOW)
    band_first = band & ((ki >= BLOCK) | jnp.logical_not(seq_start))
    sink = jnp.concatenate(
        [jnp.full((BLOCK, 1), sink_ref[h * Q_PER_KV + g], F32) for g in range(Q_PER_KV)], axis=0)
    scale = HEAD_DIM ** -0.5
    for n in range(ATT_ROWS // BLOCK):
        lo = n * BLOCK
        q4 = jnp.concatenate(
            [q_ref[lo:lo + BLOCK, g * HEAD_DIM:(g + 1) * HEAD_DIM] for g in range(Q_PER_KV)], axis=0)
        if n == 0:
            kk = jnp.concatenate([kp_ref[...], kc_ref[0:BLOCK, :]], axis=0)
            vv = jnp.concatenate([vp_ref[...], vc_ref[0:BLOCK, :]], axis=0)
            valid = band_first
        else:
            kk = kc_ref[lo - BLOCK:lo + BLOCK, :]
            vv = vc_ref[lo - BLOCK:lo + BLOCK, :]
            valid = band
        s = lax.dot_general(q4, kk, (((1,), (1,)), ((), ())), preferred_element_type=F32) * scale
        s = jnp.where(valid, s, -jnp.inf)
        m = jnp.maximum(jnp.max(s, axis=-1, keepdims=True), sink)
        p = jnp.exp(s - m)
        denom = jnp.sum(p, axis=-1, keepdims=True) + jnp.exp(sink - m)
        o = jnp.dot(p.astype(BF16), vv, preferred_element_type=F32) / denom
        for g in range(Q_PER_KV):
            o_ref[lo:lo + BLOCK, g * HEAD_DIM:(g + 1) * HEAD_DIM] = (
                o[g * BLOCK:(g + 1) * BLOCK, :].astype(o_ref.dtype))


def _attention(qkv, sinks, seq):
    t = qkv.shape[0]
    bps = seq // ATT_ROWS
    qw = Q_PER_KV * HEAD_DIM
    k_col = ATTN_WIDTH // HEAD_DIM
    v_col = (ATTN_WIDTH + KV_WIDTH) // HEAD_DIM
    sub = ATT_ROWS // BLOCK

    def prev(r):
        return jnp.maximum(r * sub - 1, 0)

    return pl.pallas_call(
        functools.partial(_attn_kernel, blocks_per_seq=bps),
        grid=(t // ATT_ROWS, N_KV_HEADS),
        in_specs=[pl.BlockSpec(memory_space=pltpu.SMEM),
                  pl.BlockSpec((ATT_ROWS, qw), lambda r, h: (r, h)),
                  pl.BlockSpec((ATT_ROWS, HEAD_DIM), lambda r, h: (r, k_col + h)),
                  pl.BlockSpec((BLOCK, HEAD_DIM), lambda r, h: (prev(r), k_col + h)),
                  pl.BlockSpec((ATT_ROWS, HEAD_DIM), lambda r, h: (r, v_col + h)),
                  pl.BlockSpec((BLOCK, HEAD_DIM), lambda r, h: (prev(r), v_col + h))],
        out_specs=pl.BlockSpec((ATT_ROWS, qw), lambda r, h: (r, h)),
        out_shape=jax.ShapeDtypeStruct((t, ATTN_WIDTH), BF16),
        compiler_params=_params("arbitrary", "arbitrary"),
        name="swa_attention",
    )(sinks, qkv, qkv, qkv, qkv, qkv)


SGU_ROWS = 512


def _sgu_kernel(u_ref, v_ref, lng_ref, lnb_ref, w_ref, b_ref, o_ref):
    ti = lax.broadcasted_iota(jnp.int32, (SGU_CHUNK, SGU_CHUNK), 0)
    si = lax.broadcasted_iota(jnp.int32, (SGU_CHUNK, SGU_CHUNK), 1)
    causal = si <= ti
    gw = SGU_WIDTH // SGU_GROUPS
    for c in range(SGU_ROWS // SGU_CHUNK):
        lo = c * SGU_CHUNK
        v = v_ref[lo:lo + SGU_CHUNK, :].astype(F32)
        mu = jnp.mean(v, axis=-1, keepdims=True)
        vc = v - mu
        vn = vc * lax.rsqrt(jnp.mean(vc * vc, axis=-1, keepdims=True) + LN_EPS)
        vn = (vn * lng_ref[...] + lnb_ref[...]).astype(BF16)
        for g in range(SGU_GROUPS):
            w = jnp.where(causal, w_ref[g], 0.0).astype(BF16)
            mixed = jnp.dot(w, vn[:, g * gw:(g + 1) * gw], preferred_element_type=F32) + b_ref[g]
            u = u_ref[lo:lo + SGU_CHUNK, g * gw:(g + 1) * gw].astype(F32)
            o_ref[lo:lo + SGU_CHUNK, g * gw:(g + 1) * gw] = (u * mixed).astype(o_ref.dtype)


def _sgu(uv, ln_g, ln_b, w_s, b_s):
    t = uv.shape[0]
    b_bcast = jnp.broadcast_to(b_s[:, :, None], (SGU_GROUPS, SGU_CHUNK, SGU_WIDTH // SGU_GROUPS))
    full3 = lambda i: (0, 0, 0)
    return pl.pallas_call(
        _sgu_kernel,
        grid=(t // SGU_ROWS,),
        in_specs=[pl.BlockSpec((SGU_ROWS, SGU_WIDTH), lambda i: (i, 0)),
                  pl.BlockSpec((SGU_ROWS, SGU_WIDTH), lambda i: (i, 1)),
                  pl.BlockSpec((1, SGU_WIDTH), lambda i: (0, 0)),
                  pl.BlockSpec((1, SGU_WIDTH), lambda i: (0, 0)),
                  pl.BlockSpec(w_s.shape, full3),
                  pl.BlockSpec(b_bcast.shape, full3)],
        out_specs=pl.BlockSpec((SGU_ROWS, SGU_WIDTH), lambda i: (i, 0)),
        out_shape=jax.ShapeDtypeStruct((t, SGU_WIDTH), BF16),
        compiler_params=_params("arbitrary"),
        name="sgu",
    )(uv, uv, ln_g.reshape(1, -1), ln_b.reshape(1, -1), w_s, b_bcast)


def _merge_kernel(a_ref, b_ref, ga_ref, gb_ref, wa_ref, wb_ref, o_ref, wabf_ref, wbbf_ref):
    _cast_weight_once(wa_ref, wabf_ref)
    _cast_weight_once(wb_ref, wbbf_ref)
    ya = jnp.dot(a_ref[...], wabf_ref[...], preferred_element_type=F32)
    yb = jnp.dot(b_ref[...], wbbf_ref[...], preferred_element_type=F32)
    o_ref[...] = (ga_ref[...].astype(F32) * ya + gb_ref[...].astype(F32) * yb).astype(o_ref.dtype)


def _merge(sgu_act, attn_act, gates, w_proj_sgu, w_proj_attn):
    t, k = sgu_act.shape
    n = w_proj_sgu.shape[1]
    nb = n // PROJ_BN
    return pl.pallas_call(
        _merge_kernel,
        grid=(nb, t // PROJ_BM),
        in_specs=[pl.BlockSpec((PROJ_BM, k), lambda j, i: (i, 0)),
                  pl.BlockSpec((PROJ_BM, k), lambda j, i: (i, 0)),
                  pl.BlockSpec((PROJ_BM, PROJ_BN), lambda j, i: (i, j)),
                  pl.BlockSpec((PROJ_BM, PROJ_BN), lambda j, i: (i, j + nb)),
                  pl.BlockSpec((k, PROJ_BN), lambda j, i: (0, j)),
                  pl.BlockSpec((k, PROJ_BN), lambda j, i: (0, j))],
        out_specs=pl.BlockSpec((PROJ_BM, PROJ_BN), lambda j, i: (i, j)),
        out_shape=jax.ShapeDtypeStruct((t, n), BF16),
        scratch_shapes=[pltpu.VMEM((k, PROJ_BN), BF16), pltpu.VMEM((k, PROJ_BN), BF16)],
        compiler_params=_params("arbitrary", "arbitrary"),
        name="proj_merge",
    )(sgu_act, attn_act, gates, gates, w_proj_sgu, w_proj_attn)


def _plain_kernel(h_ref, w_ref, o_ref, wbf_ref):
    _cast_weight_once(w_ref, wbf_ref)
    o_ref[...] = jnp.dot(h_ref[...], wbf_ref[...], preferred_element_type=F32)


def _out_proj(merged, w_out):
    t, k = merged.shape
    n = w_out.shape[1]
    return pl.pallas_call(
        _plain_kernel,
        grid=(n // PROJ_BN, t // PROJ_BM),
        in_specs=[pl.BlockSpec((PROJ_BM, k), lambda j, i: (i, 0)),
                  pl.BlockSpec((k, PROJ_BN), lambda j, i: (0, j))],
        out_specs=pl.BlockSpec((PROJ_BM, PROJ_BN), lambda j, i: (i, j)),
        out_shape=jax.ShapeDtypeStruct((t, n), F32),
        scratch_shapes=[pltpu.VMEM((k, PROJ_BN), BF16)],
        compiler_params=_params("arbitrary", "arbitrary"),
        name="out_proj",
    )(merged, w_out)


RES_BM = 256


def _mid_kernel(x_ref, t_ref, gpost_ref, gt_ref, gpre_ref, sh_ref, sc_ref, x1_ref, h2_ref):
    x1 = x_ref[...] + gt_ref[...] * _rms(t_ref[...], gpost_ref[...])
    x1_ref[...] = x1
    h2_ref[...] = (_rms(x1, gpre_ref[...]) * (1.0 + sc_ref[...]) + sh_ref[...]).astype(h2_ref.dtype)


def _final_kernel(x_ref, f_ref, gpost_ref, gt_ref, o_ref):
    o_ref[...] = x_ref[...] + gt_ref[...] * _rms(f_ref[...], gpost_ref[...])


def _res_mod_spec(blocks_per_batch, m):
    return pl.BlockSpec((None, 1, D_MODEL),
                        lambda i: ((i // blocks_per_batch) * N_MOD + m, 0, 0))


def _mid_epilogue(x2, tproj, g_post, g_pre, mod3, seq):
    t, d = x2.shape
    bpb = seq // RES_BM
    row = pl.BlockSpec((RES_BM, d), lambda i: (i, 0))
    vec = pl.BlockSpec((1, d), lambda i: (0, 0))
    return pl.pallas_call(
        _mid_kernel,
        grid=(t // RES_BM,),
        in_specs=[row, row, vec, _res_mod_spec(bpb, 2), vec,
                  _res_mod_spec(bpb, 3), _res_mod_spec(bpb, 4)],
        out_specs=[row, row],
        out_shape=[jax.ShapeDtypeStruct((t, d), F32), jax.ShapeDtypeStruct((t, d), BF16)],
        compiler_params=_params("arbitrary"),
        name="mid_epilogue",
    )(x2, tproj, g_post.reshape(1, d), mod3, g_pre.reshape(1, d), mod3, mod3)


def _final_epilogue(x1, f, g_post, mod3, seq):
    t, d = x1.shape
    bpb = seq // RES_BM
    row = pl.BlockSpec((RES_BM, d), lambda i: (i, 0))
    vec = pl.BlockSpec((1, d), lambda i: (0, 0))
    return pl.pallas_call(
        _final_kernel,
        grid=(t // RES_BM,),
        in_specs=[row, row, vec, _res_mod_spec(bpb, 5)],
        out_specs=row,
        out_shape=jax.ShapeDtypeStruct((t, d), F32),
        compiler_params=_params("arbitrary"),
        name="final_epilogue",
    )(x1, f, g_post.reshape(1, d), mod3)


FFN_BM = 512
FFN_BH = 256


def _ffn_kernel(h_ref, wg_ref, wu_ref, wd_ref, o_ref):
    h = h_ref[...]
    a = jnp.dot(h, wg_ref[...], preferred_element_type=F32)
    b = jnp.dot(h, wu_ref[...], preferred_element_type=F32)
    g = (jax.nn.silu(a) * b).astype(BF16)
    part = jnp.dot(g, wd_ref[...], preferred_element_type=F32)

    @pl.when(pl.program_id(1) == 0)
    def _():
        o_ref[...] = part

    @pl.when(pl.program_id(1) != 0)
    def _():
        o_ref[...] += part


def _ffn(h2, wg, wu, wd):
    t, d = h2.shape
    hid = wg.shape[1]
    return pl.pallas_call(
        _ffn_kernel,
        grid=(t // FFN_BM, hid // FFN_BH),
        in_specs=[pl.BlockSpec((FFN_BM, d), lambda i, j: (i, 0)),
                  pl.BlockSpec((d, FFN_BH), lambda i, j: (0, j)),
                  pl.BlockSpec((d, FFN_BH), lambda i, j: (0, j)),
                  pl.BlockSpec((FFN_BH, d), lambda i, j: (j, 0))],
        out_specs=pl.BlockSpec((FFN_BM, d), lambda i, j: (i, 0)),
        out_shape=jax.ShapeDtypeStruct((t, d), F32),
        compiler_params=_params("arbitrary", "arbitrary"),
        name="ffn",
    )(h2, wg, wu, wd)


def _rope_tables(positions):
    half = ROT_DIM // 2
    inv_freq = ROPE_THETA ** (-jnp.arange(0, ROT_DIM, 2, dtype=F32) / ROT_DIM)
    ang = positions.reshape(-1).astype(F32)[:, None] * inv_freq
    cos, sin = jnp.cos(ang), jnp.sin(ang)
    t = ang.shape[0]
    ones = jnp.ones((t, HEAD_DIM - ROT_DIM), F32)
    zeros_h = jnp.zeros((t, half), F32)
    zeros_r = jnp.zeros((t, HEAD_DIM - ROT_DIM), F32)
    c = jnp.concatenate([cos, cos, ones], axis=1)
    s1 = jnp.concatenate([-sin, zeros_h, zeros_r], axis=1)
    s2 = jnp.concatenate([zeros_h, sin, zeros_r], axis=1)
    return c, s1, s2


def kernel(x, c, positions, w_ada, b_ada, g_pre_mix, w_in, attn_sinks, sgu_ln_g, sgu_ln_b, sgu_w, sgu_b, w_proj_sgu, w_proj_attn, w_out, g_post_mix, g_pre_ffn, w_gate, w_up, w_down, g_post_ffn):
    bsz, seq, d = x.shape
    depth = w_ada.shape[0]
    t = bsz * seq
    xcur = x.reshape(t, d)
    rope_c, rope_s1, rope_s2 = _rope_tables(positions)
    c_pad = jnp.pad(c, ((0, 8 - bsz), (0, 0)))
    for l in range(depth):
        mod = _ada_mod(c_pad, w_ada[l], b_ada[l])[:bsz]
        mod3 = mod.reshape(bsz * N_MOD, 1, d)

        h1 = _prenorm(xcur, g_pre_mix[l], mod3, seq, 0, 1)
        qkv = _in_proj(_qkv_kernel, h1, w_in[l], 0, QKV_WIDTH,
                       extra=(rope_c, rope_s1, rope_s2), name="in_proj_qkv")
        uv = _in_proj(functools.partial(_act_kernel, _gelu), h1, w_in[l], QKV_WIDTH,
                      2 * SGU_WIDTH, name="in_proj_uv")
        gates = _in_proj(functools.partial(_act_kernel, jax.nn.sigmoid), h1, w_in[l],
                         QKV_WIDTH + 2 * SGU_WIDTH, 2 * d, name="in_proj_gates")

        attn_act = _attention(qkv, attn_sinks[l], seq)
        sgu_act = _sgu(uv, sgu_ln_g[l], sgu_ln_b[l], sgu_w[l], sgu_b[l])
        merged = _merge(sgu_act, attn_act, gates, w_proj_sgu[l], w_proj_attn[l])
        tproj = _out_proj(merged, w_out[l])
        x1, h2 = _mid_epilogue(xcur, tproj, g_post_mix[l], g_pre_ffn[l], mod3, seq)

        f = _ffn(h2, w_gate[l].astype(BF16), w_up[l].astype(BF16), w_down[l].astype(BF16))
        xcur = _final_epilogue(x1, f, g_post_ffn[l], mod3, seq)
    return xcur.reshape(bsz, seq, d)
```

```python
import functools

import jax
import jax.numpy as jnp
from jax import lax
from jax.experimental import pallas as pl
from jax.experimental.pallas import tpu as pltpu

D_MODEL = 4096
HEAD_DIM = 128
N_Q_HEADS = 16
N_KV_HEADS = 4
Q_PER_KV = 4
ATTN_WIDTH = N_Q_HEADS * HEAD_DIM
KV_WIDTH = N_KV_HEADS * HEAD_DIM
QKV_WIDTH = ATTN_WIDTH + 2 * KV_WIDTH
WINDOW = 128
BLOCK = 128
ROT_DIM = 32
ROPE_THETA = 500000.0
SGU_CHUNK = 128
SGU_GROUPS = 16
SGU_WIDTH = 2048
FFN_HIDDEN = 11008
N_MOD = 6
RMS_EPS = 1e-6
LN_EPS = 1e-5

VMEM_LIMIT_BYTES = 56 * 1024 * 1024

BF16 = jnp.bfloat16
F32 = jnp.float32


def _params(*semantics):
    return pltpu.CompilerParams(dimension_semantics=semantics,
                                vmem_limit_bytes=VMEM_LIMIT_BYTES)


def _rms(v, g):
    return v * lax.rsqrt(jnp.mean(v * v, axis=-1, keepdims=True) + RMS_EPS) * g


ADA_BN = 1024


def _ada_kernel(c_ref, w_ref, b_ref, o_ref):
    c_act = jax.nn.silu(c_ref[...]).astype(BF16)
    o_ref[...] = jnp.dot(c_act, w_ref[...].astype(BF16),
                         preferred_element_type=F32) + b_ref[...]


def _ada_mod(c_pad, w_ada, b_ada):
    rows, d = c_pad.shape
    n = w_ada.shape[1]
    return pl.pallas_call(
        _ada_kernel,
        grid=(n // ADA_BN,),
        in_specs=[pl.BlockSpec((rows, d), lambda j: (0, 0)),
                  pl.BlockSpec((d, ADA_BN), lambda j: (0, j)),
                  pl.BlockSpec((1, ADA_BN), lambda j: (0, j))],
        out_specs=pl.BlockSpec((rows, ADA_BN), lambda j: (0, j)),
        out_shape=jax.ShapeDtypeStruct((rows, n), F32),
        compiler_params=_params("arbitrary"),
        name="ada_mod",
    )(c_pad, w_ada, b_ada.reshape(1, n))


NORM_BM = 512


def _prenorm_kernel(x_ref, g_ref, sh_ref, sc_ref, o_ref):
    y = _rms(x_ref[...], g_ref[...])
    o_ref[...] = (y * (1.0 + sc_ref[...]) + sh_ref[...]).astype(o_ref.dtype)


def _mod_spec(blocks_per_batch, m):
    return pl.BlockSpec((None, 1, D_MODEL),
                        lambda i: ((i // blocks_per_batch) * N_MOD + m, 0, 0))


def _prenorm(x2, g, mod3, seq, m_shift, m_scale):
    t, d = x2.shape
    bpb = seq // NORM_BM
    return pl.pallas_call(
        _prenorm_kernel,
        grid=(t // NORM_BM,),
        in_specs=[pl.BlockSpec((NORM_BM, d), lambda i: (i, 0)),
                  pl.BlockSpec((1, d), lambda i: (0, 0)),
                  _mod_spec(bpb, m_shift), _mod_spec(bpb, m_scale)],
        out_specs=pl.BlockSpec((NORM_BM, d), lambda i: (i, 0)),
        out_shape=jax.ShapeDtypeStruct((t, d), BF16),
        compiler_params=_params("arbitrary"),
        name="prenorm",
    )(x2, g.reshape(1, d), mod3, mod3)


PROJ_BM = 1024
PROJ_BN = 512


def _cast_weight_once(w_ref, wbf_ref):
    @pl.when(pl.program_id(1) == 0)
    def _():
        wbf_ref[...] = w_ref[...].astype(BF16)


ROPE_ROWS = 128


def _qkv_kernel(h_ref, w_ref, c_ref, s1_ref, s2_ref, o_ref, wbf_ref, z_ref):
    _cast_weight_once(w_ref, wbf_ref)
    z_ref[...] = jnp.dot(h_ref[...], wbf_ref[...], preferred_element_type=F32)
    is_v = pl.program_id(0) >= (ATTN_WIDTH + KV_WIDTH) // PROJ_BN
    half = ROT_DIM // 2

    @pl.when(jnp.logical_not(is_v))
    def _():
        for r in range(0, PROJ_BM, ROPE_ROWS):
            c = c_ref[r:r + ROPE_ROWS, :]
            s1 = s1_ref[r:r + ROPE_ROWS, :]
            s2 = s2_ref[r:r + ROPE_ROWS, :]
            for hd in range(0, PROJ_BN, HEAD_DIM):
                z = z_ref[r:r + ROPE_ROWS, hd:hd + HEAD_DIM]
                up = pltpu.roll(z, HEAD_DIM - half, 1)
                down = pltpu.roll(z, half, 1)
                o_ref[r:r + ROPE_ROWS, hd:hd + HEAD_DIM] = (
                    z * c + up * s1 + down * s2).astype(o_ref.dtype)

    @pl.when(is_v)
    def _():
        o_ref[...] = z_ref[...].astype(o_ref.dtype)


def _act_kernel(act, h_ref, w_ref, o_ref, wbf_ref):
    _cast_weight_once(w_ref, wbf_ref)
    z = jnp.dot(h_ref[...], wbf_ref[...], preferred_element_type=F32)
    o_ref[...] = act(z).astype(o_ref.dtype)


def _gelu(z):
    return 0.5 * z * (1.0 + lax.erf(z * (0.5 ** 0.5)))


def _in_proj(kernel_fn, h, w_in, col_off, width, extra=(), extra_scratch=(), name=None):
    t, d = h.shape
    off_blocks = col_off // PROJ_BN
    extra_specs = [pl.BlockSpec((PROJ_BM, a.shape[1]), lambda j, i: (i, 0)) for a in extra]
    return pl.pallas_call(
        kernel_fn,
        grid=(width // PROJ_BN, t // PROJ_BM),
        in_specs=[pl.BlockSpec((PROJ_BM, d), lambda j, i: (i, 0)),
                  pl.BlockSpec((d, PROJ_BN), lambda j, i: (0, j + off_blocks))] + extra_specs,
        out_specs=pl.BlockSpec((PROJ_BM, PROJ_BN), lambda j, i: (i, j)),
        out_shape=jax.ShapeDtypeStruct((t, width), BF16),
        scratch_shapes=[pltpu.VMEM((d, PROJ_BN), BF16), *extra_scratch],
        compiler_params=_params("arbitrary", "arbitrary"),
        name=name,
    )(h, w_in, *extra)


ATT_ROWS = 1024


def _attn_kernel(sink_ref, q_ref, kc_ref, kp_ref, vc_ref, vp_ref, o_ref, *, blocks_per_seq):
    r = pl.program_id(0)
    h = pl.program_id(1)
    seq_start = (r % blocks_per_seq) == 0
    rows = Q_PER_KV * BLOCK
    qi = lax.broadcasted_iota(jnp.int32, (rows, 2 * BLOCK), 0) % BLOCK
    ki = lax.broadcasted_iota(jnp.int32, (rows, 2 * BLOCK), 1)
    diff = qi + BLOCK - ki
    band = (diff >= 0) & (diff < WINDOW)
    band_first = band & ((ki >= BLOCK) | jnp.logical_not(seq_start))
    sink = jnp.concatenate(
        [jnp.full((BLOCK, 1), sink_ref[h * Q_PER_KV + g], F32) for g in range(Q_PER_KV)], axis=0)
    scale = HEAD_DIM ** -0.5
    for n in range(ATT_ROWS // BLOCK):
        lo = n * BLOCK
        q4 = jnp.concatenate(
            [q_ref[lo:lo + BLOCK, g * HEAD_DIM:(g + 1) * HEAD_DIM] for g in range(Q_PER_KV)], axis=0)
        if n == 0:
            kk = jnp.concatenate([kp_ref[...], kc_ref[0:BLOCK, :]], axis=0)
            vv = jnp.concatenate([vp_ref[...], vc_ref[0:BLOCK, :]], axis=0)
            valid = band_first
        else:
            kk = kc_ref[lo - BLOCK:lo + BLOCK, :]
            vv = vc_ref[lo - BLOCK:lo + BLOCK, :]
            valid = band
        s = lax.dot_general(q4, kk, (((1,), (1,)), ((), ())), preferred_element_type=F32) * scale
        s = jnp.where(valid, s, -jnp.inf)
        m = jnp.maximum(jnp.max(s, axis=-1, keepdims=True), sink)
        p = jnp.exp(s - m)
        denom = jnp.sum(p, axis=-1, keepdims=True) + jnp.exp(sink - m)
        o = jnp.dot(p.astype(BF16), vv, preferred_element_type=F32) / denom
        for g in range(Q_PER_KV):
            o_ref[lo:lo + BLOCK, g * HEAD_DIM:(g + 1) * HEAD_DIM] = (
                o[g * BLOCK:(g + 1) * BLOCK, :].astype(o_ref.dtype))


def _attention(qkv, sinks, seq):
    t = qkv.shape[0]
    bps = seq // ATT_ROWS
    qw = Q_PER_KV * HEAD_DIM
    k_col = ATTN_WIDTH // HEAD_DIM
    v_col = (ATTN_WIDTH + KV_WIDTH) // HEAD_DIM
    sub = ATT_ROWS // BLOCK

    def prev(r):
        return jnp.maximum(r * sub - 1, 0)

    return pl.pallas_call(
        functools.partial(_attn_kernel, blocks_per_seq=bps),
        grid=(t // ATT_ROWS, N_KV_HEADS),
        in_specs=[pl.BlockSpec(memory_space=pltpu.SMEM),
                  pl.BlockSpec((ATT_ROWS, qw), lambda r, h: (r, h)),
                  pl.BlockSpec((ATT_ROWS, HEAD_DIM), lambda r, h: (r, k_col + h)),
                  pl.BlockSpec((BLOCK, HEAD_DIM), lambda r, h: (prev(r), k_col + h)),
                  pl.BlockSpec((ATT_ROWS, HEAD_DIM), lambda r, h: (r, v_col + h)),
                  pl.BlockSpec((BLOCK, HEAD_DIM), lambda r, h: (prev(r), v_col + h))],
        out_specs=pl.BlockSpec((ATT_ROWS, qw), lambda r, h: (r, h)),
        out_shape=jax.ShapeDtypeStruct((t, ATTN_WIDTH), BF16),
        compiler_params=_params("arbitrary", "arbitrary"),
        name="swa_attention",
    )(sinks, qkv, qkv, qkv, qkv, qkv)


SGU_ROWS = 512


def _sgu_kernel(u_ref, v_ref, lng_ref, lnb_ref, w_ref, b_ref, o_ref):
    ti = lax.broadcasted_iota(jnp.int32, (SGU_CHUNK, SGU_CHUNK), 0)
    si = lax.broadcasted_iota(jnp.int32, (SGU_CHUNK, SGU_CHUNK), 1)
    causal = si <= ti
    gw = SGU_WIDTH // SGU_GROUPS
    for c in range(SGU_ROWS // SGU_CHUNK):
        lo = c * SGU_CHUNK
        v = v_ref[lo:lo + SGU_CHUNK, :].astype(F32)
        mu = jnp.mean(v, axis=-1, keepdims=True)
        vc = v - mu
        vn = vc * lax.rsqrt(jnp.mean(vc * vc, axis=-1, keepdims=True) + LN_EPS)
        vn = (vn * lng_ref[...] + lnb_ref[...]).astype(BF16)
        for g in range(SGU_GROUPS):
            w = jnp.where(causal, w_ref[g], 0.0).astype(BF16)
            mixed = jnp.dot(w, vn[:, g * gw:(g + 1) * gw], preferred_element_type=F32) + b_ref[g]
            u = u_ref[lo:lo + SGU_CHUNK, g * gw:(g + 1) * gw].astype(F32)
            o_ref[lo:lo + SGU_CHUNK, g * gw:(g + 1) * gw] = (u * mixed).astype(o_ref.dtype)


def _sgu(uv, ln_g, ln_b, w_s, b_s):
    t = uv.shape[0]
    b_bcast = jnp.broadcast_to(b_s[:, :, None], (SGU_GROUPS, SGU_CHUNK, SGU_WIDTH // SGU_GROUPS))
    full3 = lambda i: (0, 0, 0)
    return pl.pallas_call(
        _sgu_kernel,
        grid=(t // SGU_ROWS,),
        in_specs=[pl.BlockSpec((SGU_ROWS, SGU_WIDTH), lambda i: (i, 0)),
                  pl.BlockSpec((SGU_ROWS, SGU_WIDTH), lambda i: (i, 1)),
                  pl.BlockSpec((1, SGU_WIDTH), lambda i: (0, 0)),
                  pl.BlockSpec((1, SGU_WIDTH), lambda i: (0, 0)),
                  pl.BlockSpec(w_s.shape, full3),
                  pl.BlockSpec(b_bcast.shape, full3)],
        out_specs=pl.BlockSpec((SGU_ROWS, SGU_WIDTH), lambda i: (i, 0)),
        out_shape=jax.ShapeDtypeStruct((t, SGU_WIDTH), BF16),
        compiler_params=_params("arbitrary"),
        name="sgu",
    )(uv, uv, ln_g.reshape(1, -1), ln_b.reshape(1, -1), w_s, b_bcast)


def _merge_kernel(a_ref, b_ref, ga_ref, gb_ref, wa_ref, wb_ref, o_ref, wabf_ref, wbbf_ref):
    _cast_weight_once(wa_ref, wabf_ref)
    _cast_weight_once(wb_ref, wbbf_ref)
    ya = jnp.dot(a_ref[...], wabf_ref[...], preferred_element_type=F32)
    yb = jnp.dot(b_ref[...], wbbf_ref[...], preferred_element_type=F32)
    o_ref[...] = (ga_ref[...].astype(F32) * ya + gb_ref[...].astype(F32) * yb).astype(o_ref.dtype)


def _merge(sgu_act, attn_act, gates, w_proj_sgu, w_proj_attn):
    t, k = sgu_act.shape
    n = w_proj_sgu.shape[1]
    nb = n // PROJ_BN
    return pl.pallas_call(
        _merge_kernel,
        grid=(nb, t // PROJ_BM),
        in_specs=[pl.BlockSpec((PROJ_BM, k), lambda j, i: (i, 0)),
                  pl.BlockSpec((PROJ_BM, k), lambda j, i: (i, 0)),
                  pl.BlockSpec((PROJ_BM, PROJ_BN), lambda j, i: (i, j)),
                  pl.BlockSpec((PROJ_BM, PROJ_BN), lambda j, i: (i, j + nb)),
                  pl.BlockSpec((k, PROJ_BN), lambda j, i: (0, j)),
                  pl.BlockSpec((k, PROJ_BN), lambda j, i: (0, j))],
        out_specs=pl.BlockSpec((PROJ_BM, PROJ_BN), lambda j, i: (i, j)),
        out_shape=jax.ShapeDtypeStruct((t, n), BF16),
        scratch_shapes=[pltpu.VMEM((k, PROJ_BN), BF16), pltpu.VMEM((k, PROJ_BN), BF16)],
        compiler_params=_params("arbitrary", "arbitrary"),
        name="proj_merge",
    )(sgu_act, attn_act, gates, gates, w_proj_sgu, w_proj_attn)


def _plain_kernel(h_ref, w_ref, o_ref, wbf_ref):
    _cast_weight_once(w_ref, wbf_ref)
    o_ref[...] = jnp.dot(h_ref[...], wbf_ref[...], preferred_element_type=F32)


def _out_proj(merged, w_out):
    t, k = merged.shape
    n = w_out.shape[1]
    return pl.pallas_call(
        _plain_kernel,
        grid=(n // PROJ_BN, t // PROJ_BM),
        in_specs=[pl.BlockSpec((PROJ_BM, k), lambda j, i: (i, 0)),
                  pl.BlockSpec((k, PROJ_BN), lambda j, i: (0, j))],
        out_specs=pl.BlockSpec((PROJ_BM, PROJ_BN), lambda j, i: (i, j)),
        out_shape=jax.ShapeDtypeStruct((t, n), F32),
        scratch_shapes=[pltpu.VMEM((k, PROJ_BN), BF16)],
        compiler_params=_params("arbitrary", "arbitrary"),
        name="out_proj",
    )(merged, w_out)


RES_BM = 256


def _mid_kernel(x_ref, t_ref, gpost_ref, gt_ref, gpre_ref, sh_ref, sc_ref, x1_ref, h2_ref):
    x1 = x_ref[...] + gt_ref[...] * _rms(t_ref[...], gpost_ref[...])
    x1_ref[...] = x1
    h2_ref[...] = (_rms(x1, gpre_ref[...]) * (1.0 + sc_ref[...]) + sh_ref[...]).astype(h2_ref.dtype)


def _final_kernel(x_ref, f_ref, gpost_ref, gt_ref, o_ref):
    o_ref[...] = x_ref[...] + gt_ref[...] * _rms(f_ref[...], gpost_ref[...])


def _res_mod_spec(blocks_per_batch, m):
    return pl.BlockSpec((None, 1, D_MODEL),
                        lambda i: ((i // blocks_per_batch) * N_MOD + m, 0, 0))


def _mid_epilogue(x2, tproj, g_post, g_pre, mod3, seq):
    t, d = x2.shape
    bpb = seq // RES_BM
    row = pl.BlockSpec((RES_BM, d), lambda i: (i, 0))
    vec = pl.BlockSpec((1, d), lambda i: (0, 0))
    return pl.pallas_call(
        _mid_kernel,
        grid=(t // RES_BM,),
        in_specs=[row, row, vec, _res_mod_spec(bpb, 2), vec,
                  _res_mod_spec(bpb, 3), _res_mod_spec(bpb, 4)],
        out_specs=[row, row],
        out_shape=[jax.ShapeDtypeStruct((t, d), F32), jax.ShapeDtypeStruct((t, d), BF16)],
        compiler_params=_params("arbitrary"),
        name="mid_epilogue",
    )(x2, tproj, g_post.reshape(1, d), mod3, g_pre.reshape(1, d), mod3, mod3)


def _final_epilogue(x1, f, g_post, mod3, seq):
    t, d = x1.shape
    bpb = seq // RES_BM
    row = pl.BlockSpec((RES_BM, d), lambda i: (i, 0))
    vec = pl.BlockSpec((1, d), lambda i: (0, 0))
    return pl.pallas_call(
        _final_kernel,
        grid=(t // RES_BM,),
        in_specs=[row, row, vec, _res_mod_spec(bpb, 5)],
        out_specs=row,
        out_shape=jax.ShapeDtypeStruct((t, d), F32),
        compiler_params=_params("arbitrary"),
        name="final_epilogue",
    )(x1, f, g_post.reshape(1, d), mod3)


FFN_BM = 512
FFN_BH = 256


def _ffn_kernel(h_ref, wgu_ref, wd_ref, o_ref):
    @pl.when(pl.program_id(1) == 0)
    def _():
        o_ref[...] = jnp.zeros_like(o_ref)

    ab = jnp.dot(h_ref[...], wgu_ref[...], preferred_element_type=F32)
    g = (jax.nn.silu(ab[:, :FFN_BH]) * ab[:, FFN_BH:]).astype(BF16)
    o_ref[...] += jnp.dot(g, wd_ref[...], preferred_element_type=F32)


def _ffn_gate_up_blocks(w_gate, w_up):
    d, hid = w_gate.shape
    nb = hid // FFN_BH
    wg = w_gate.astype(BF16).reshape(d, nb, FFN_BH)
    wu = w_up.astype(BF16).reshape(d, nb, FFN_BH)
    return jnp.concatenate([wg, wu], axis=2).transpose(1, 0, 2)


def _ffn(h2, wgu, wd):
    t, d = h2.shape
    hid = wd.shape[0]
    return pl.pallas_call(
        _ffn_kernel,
        grid=(t // FFN_BM, hid // FFN_BH),
        in_specs=[pl.BlockSpec((FFN_BM, d), lambda i, j: (i, 0)),
                  pl.BlockSpec((None, d, 2 * FFN_BH), lambda i, j: (j, 0, 0)),
                  pl.BlockSpec((FFN_BH, d), lambda i, j: (j, 0))],
        out_specs=pl.BlockSpec((FFN_BM, d), lambda i, j: (i, 0)),
        out_shape=jax.ShapeDtypeStruct((t, d), F32),
        compiler_params=_params("arbitrary", "arbitrary"),
        name="ffn",
    )(h2, wgu, wd)


def _rope_tables(positions):
    half = ROT_DIM // 2
    inv_freq = ROPE_THETA ** (-jnp.arange(0, ROT_DIM, 2, dtype=F32) / ROT_DIM)
    ang = positions.reshape(-1).astype(F32)[:, None] * inv_freq
    cos, sin = jnp.cos(ang), jnp.sin(ang)
    t = ang.shape[0]
    ones = jnp.ones((t, HEAD_DIM - ROT_DIM), F32)
    zeros_h = jnp.zeros((t, half), F32)
    zeros_r = jnp.zeros((t, HEAD_DIM - ROT_DIM), F32)
    c = jnp.concatenate([cos, cos, ones], axis=1)
    s1 = jnp.concatenate([-sin, zeros_h, zeros_r], axis=1)
    s2 = jnp.concatenate([zeros_h, sin, zeros_r], axis=1)
    return c, s1, s2


def kernel(x, c, positions, w_ada, b_ada, g_pre_mix, w_in, attn_sinks, sgu_ln_g, sgu_ln_b, sgu_w, sgu_b, w_proj_sgu, w_proj_attn, w_out, g_post_mix, g_pre_ffn, w_gate, w_up, w_down, g_post_ffn):
    bsz, seq, d = x.shape
    depth = w_ada.shape[0]
    t = bsz * seq
    xcur = x.reshape(t, d)
    rope_c, rope_s1, rope_s2 = _rope_tables(positions)
    c_pad = jnp.pad(c, ((0, 8 - bsz), (0, 0)))
    for l in range(depth):
        mod = _ada_mod(c_pad, w_ada[l], b_ada[l])[:bsz]
        mod3 = mod.reshape(bsz * N_MOD, 1, d)

        h1 = _prenorm(xcur, g_pre_mix[l], mod3, seq, 0, 1)
        qkv = _in_proj(_qkv_kernel, h1, w_in[l], 0, QKV_WIDTH,
                       extra=(rope_c, rope_s1, rope_s2),
                       extra_scratch=(pltpu.VMEM((PROJ_BM, PROJ_BN), F32),), name="in_proj_qkv")
        uv = _in_proj(functools.partial(_act_kernel, _gelu), h1, w_in[l], QKV_WIDTH,
                      2 * SGU_WIDTH, name="in_proj_uv")
        gates = _in_proj(functools.partial(_act_kernel, jax.nn.sigmoid), h1, w_in[l],
                         QKV_WIDTH + 2 * SGU_WIDTH, 2 * d, name="in_proj_gates")

        attn_act = _attention(qkv, attn_sinks[l], seq)
        sgu_act = _sgu(uv, sgu_ln_g[l], sgu_ln_b[l], sgu_w[l], sgu_b[l])
        merged = _merge(sgu_act, attn_act, gates, w_proj_sgu[l], w_proj_attn[l])
        tproj = _out_proj(merged, w_out[l])
        x1, h2 = _mid_epilogue(xcur, tproj, g_post_mix[l], g_pre_ffn[l], mod3, seq)

        f = _ffn(h2, _ffn_gate_up_blocks(w_gate[l], w_up[l]), w_down[l].astype(BF16))
        xcur = _final_epilogue(x1, f, g_post_ffn[l], mod3, seq)
    return xcur.reshape(bsz, seq, d)
```

```python
import functools

import jax
import jax.numpy as jnp
from jax import lax
from jax.experimental import pallas as pl
from jax.experimental.pallas import tpu as pltpu

D_MODEL = 4096
HEAD_DIM = 128
N_Q_HEADS = 16
N_KV_HEADS = 4
Q_PER_KV = 4
ATTN_WIDTH = N_Q_HEADS * HEAD_DIM
KV_WIDTH = N_KV_HEADS * HEAD_DIM
QKV_WIDTH = ATTN_WIDTH + 2 * KV_WIDTH
WINDOW = 128
BLOCK = 128
ROT_DIM = 32
ROPE_THETA = 500000.0
SGU_CHUNK = 128
SGU_GROUPS = 16
SGU_WIDTH = 2048
FFN_HIDDEN = 11008
N_MOD = 6
RMS_EPS = 1e-6
LN_EPS = 1e-5

VMEM_LIMIT_BYTES = 56 * 1024 * 1024

BF16 = jnp.bfloat16
F32 = jnp.float32


def _params(*semantics):
    return pltpu.CompilerParams(dimension_semantics=semantics,
                                vmem_limit_bytes=VMEM_LIMIT_BYTES)


def _rms(v, g):
    return v * lax.rsqrt(jnp.mean(v * v, axis=-1, keepdims=True) + RMS_EPS) * g


ADA_BN = 1024


def _ada_kernel(c_ref, w_ref, b_ref, o_ref):
    c_act = jax.nn.silu(c_ref[...]).astype(BF16)
    o_ref[...] = jnp.dot(c_act, w_ref[...].astype(BF16),
                         preferred_element_type=F32) + b_ref[...]


def _ada_mod(c_pad, w_ada, b_ada):
    rows, d = c_pad.shape
    n = w_ada.shape[1]
    return pl.pallas_call(
        _ada_kernel,
        grid=(n // ADA_BN,),
        in_specs=[pl.BlockSpec((rows, d), lambda j: (0, 0)),
                  pl.BlockSpec((d, ADA_BN), lambda j: (0, j)),
                  pl.BlockSpec((1, ADA_BN), lambda j: (0, j))],
        out_specs=pl.BlockSpec((rows, ADA_BN), lambda j: (0, j)),
        out_shape=jax.ShapeDtypeStruct((rows, n), F32),
        compiler_params=_params("arbitrary"),
        name="ada_mod",
    )(c_pad, w_ada, b_ada.reshape(1, n))


NORM_BM = 512


def _prenorm_kernel(x_ref, g_ref, sh_ref, sc_ref, o_ref):
    y = _rms(x_ref[...], g_ref[...])
    o_ref[...] = (y * (1.0 + sc_ref[...]) + sh_ref[...]).astype(o_ref.dtype)


def _mod_spec(blocks_per_batch, m):
    return pl.BlockSpec((None, 1, D_MODEL),
                        lambda i: ((i // blocks_per_batch) * N_MOD + m, 0, 0))


def _prenorm(x2, g, mod3, seq, m_shift, m_scale):
    t, d = x2.shape
    bpb = seq // NORM_BM
    return pl.pallas_call(
        _prenorm_kernel,
        grid=(t // NORM_BM,),
        in_specs=[pl.BlockSpec((NORM_BM, d), lambda i: (i, 0)),
                  pl.BlockSpec((1, d), lambda i: (0, 0)),
                  _mod_spec(bpb, m_shift), _mod_spec(bpb, m_scale)],
        out_specs=pl.BlockSpec((NORM_BM, d), lambda i: (i, 0)),
        out_shape=jax.ShapeDtypeStruct((t, d), BF16),
        compiler_params=_params("arbitrary"),
        name="prenorm",
    )(x2, g.reshape(1, d), mod3, mod3)


PROJ_BM = 1024
PROJ_BN = 512


def _cast_weight_once(w_ref, wbf_ref):
    @pl.when(pl.program_id(1) == 0)
    def _():
        wbf_ref[...] = w_ref[...].astype(BF16)


ROPE_ROWS = 128


def _qkv_kernel(h_ref, w_ref, c_ref, s1_ref, s2_ref, o_ref, wbf_ref, z_ref):
    _cast_weight_once(w_ref, wbf_ref)
    z_ref[...] = jnp.dot(h_ref[...], wbf_ref[...], preferred_element_type=F32)
    is_v = pl.program_id(0) >= (ATTN_WIDTH + KV_WIDTH) // PROJ_BN
    half = ROT_DIM // 2

    @pl.when(jnp.logical_not(is_v))
    def _():
        for r in range(0, PROJ_BM, ROPE_ROWS):
            c = c_ref[r:r + ROPE_ROWS, :]
            s1 = s1_ref[r:r + ROPE_ROWS, :]
            s2 = s2_ref[r:r + ROPE_ROWS, :]
            for hd in range(0, PROJ_BN, HEAD_DIM):
                z = z_ref[r:r + ROPE_ROWS, hd:hd + HEAD_DIM]
                up = pltpu.roll(z, HEAD_DIM - half, 1)
                down = pltpu.roll(z, half, 1)
                o_ref[r:r + ROPE_ROWS, hd:hd + HEAD_DIM] = (
                    z * c + up * s1 + down * s2).astype(o_ref.dtype)

    @pl.when(is_v)
    def _():
        o_ref[...] = z_ref[...].astype(o_ref.dtype)


def _act_kernel(act, h_ref, w_ref, o_ref, wbf_ref):
    _cast_weight_once(w_ref, wbf_ref)
    z = jnp.dot(h_ref[...], wbf_ref[...], preferred_element_type=F32)
    o_ref[...] = act(z).astype(o_ref.dtype)


def _gelu(z):
    return 0.5 * z * (1.0 + lax.erf(z * (0.5 ** 0.5)))


def _in_proj(kernel_fn, h, w_in, col_off, width, extra=(), extra_scratch=(), name=None):
    t, d = h.shape
    off_blocks = col_off // PROJ_BN
    extra_specs = [pl.BlockSpec((PROJ_BM, a.shape[1]), lambda j, i: (i, 0)) for a in extra]
    return pl.pallas_call(
        kernel_fn,
        grid=(width // PROJ_BN, t // PROJ_BM),
        in_specs=[pl.BlockSpec((PROJ_BM, d), lambda j, i: (i, 0)),
                  pl.BlockSpec((d, PROJ_BN), lambda j, i: (0, j + off_blocks))] + extra_specs,
        out_specs=pl.BlockSpec((PROJ_BM, PROJ_BN), lambda j, i: (i, j)),
        out_shape=jax.ShapeDtypeStruct((t, width), BF16),
        scratch_shapes=[pltpu.VMEM((d, PROJ_BN), BF16), *extra_scratch],
        compiler_params=_params("arbitrary", "arbitrary"),
        name=name,
    )(h, w_in, *extra)


ATT_ROWS = 1024


def _attn_kernel(sink_ref, q_ref, kc_ref, kp_ref, vc_ref, vp_ref, o_ref, *, blocks_per_seq):
    r = pl.program_id(0)
    h = pl.program_id(1)
    seq_start = (r % blocks_per_seq) == 0
    rows = Q_PER_KV * BLOCK
    qi = lax.broadcasted_iota(jnp.int32, (rows, 2 * BLOCK), 0) % BLOCK
    ki = lax.broadcasted_iota(jnp.int32, (rows, 2 * BLOCK), 1)
    diff = qi + BLOCK - ki
    band = (diff >= 0) & (diff < WINDOW)
    band_first = band & ((ki >= BLOCK) | jnp.logical_not(seq_start))
    sink = jnp.concatenate(
        [jnp.full((BLOCK, 1), sink_ref[h * Q_PER_KV + g], F32) for g in range(Q_PER_KV)], axis=0)
    scale = HEAD_DIM ** -0.5
    for n in range(ATT_ROWS // BLOCK):
        lo = n * BLOCK
        q4 = jnp.concatenate(
            [q_ref[lo:lo + BLOCK, g * HEAD_DIM:(g + 1) * HEAD_DIM] for g in range(Q_PER_KV)], axis=0)
        if n == 0:
            kk = jnp.concatenate([kp_ref[...], kc_ref[0:BLOCK, :]], axis=0)
            vv = jnp.concatenate([vp_ref[...], vc_ref[0:BLOCK, :]], axis=0)
            valid = band_first
        else:
            kk = kc_ref[lo - BLOCK:lo + BLOCK, :]
            vv = vc_ref[lo - BLOCK:lo + BLOCK, :]
            valid = band
        s = lax.dot_general(q4, kk, (((1,), (1,)), ((), ())), preferred_element_type=F32) * scale
        s = jnp.where(valid, s, -jnp.inf)
        m = jnp.maximum(jnp.max(s, axis=-1, keepdims=True), sink)
        p = jnp.exp(s - m)
        denom = jnp.sum(p, axis=-1, keepdims=True) + jnp.exp(sink - m)
        o = jnp.dot(p.astype(BF16), vv, preferred_element_type=F32) / denom
        for g in range(Q_PER_KV):
            o_ref[lo:lo + BLOCK, g * HEAD_DIM:(g + 1) * HEAD_DIM] = (
                o[g * BLOCK:(g + 1) * BLOCK, :].astype(o_ref.dtype))


def _attention(qkv, sinks, seq):
    t = qkv.shape[0]
    bps = seq // ATT_ROWS
    qw = Q_PER_KV * HEAD_DIM
    k_col = ATTN_WIDTH // HEAD_DIM
    v_col = (ATTN_WIDTH + KV_WIDTH) // HEAD_DIM
    sub = ATT_ROWS // BLOCK

    def prev(r):
        return jnp.maximum(r * sub - 1, 0)

    return pl.pallas_call(
        functools.partial(_attn_kernel, blocks_per_seq=bps),
        grid=(t // ATT_ROWS, N_KV_HEADS),
        in_specs=[pl.BlockSpec(memory_space=pltpu.SMEM),
                  pl.BlockSpec((ATT_ROWS, qw), lambda r, h: (r, h)),
                  pl.BlockSpec((ATT_ROWS, HEAD_DIM), lambda r, h: (r, k_col + h)),
                  pl.BlockSpec((BLOCK, HEAD_DIM), lambda r, h: (prev(r), k_col + h)),
                  pl.BlockSpec((ATT_ROWS, HEAD_DIM), lambda r, h: (r, v_col + h)),
                  pl.BlockSpec((BLOCK, HEAD_DIM), lambda r, h: (prev(r), v_col + h))],
        out_specs=pl.BlockSpec((ATT_ROWS, qw), lambda r, h: (r, h)),
        out_shape=jax.ShapeDtypeStruct((t, ATTN_WIDTH), BF16),
        compiler_params=_params("arbitrary", "arbitrary"),
        name="swa_attention",
    )(sinks, qkv, qkv, qkv, qkv, qkv)


SGU_ROWS = 512


def _sgu_kernel(u_ref, v_ref, lng_ref, lnb_ref, w_ref, b_ref, o_ref):
    ti = lax.broadcasted_iota(jnp.int32, (SGU_CHUNK, SGU_CHUNK), 0)
    si = lax.broadcasted_iota(jnp.int32, (SGU_CHUNK, SGU_CHUNK), 1)
    causal = si <= ti
    gw = SGU_WIDTH // SGU_GROUPS
    for c in range(SGU_ROWS // SGU_CHUNK):
        lo = c * SGU_CHUNK
        v = v_ref[lo:lo + SGU_CHUNK, :].astype(F32)
        mu = jnp.mean(v, axis=-1, keepdims=True)
        vc = v - mu
        vn = vc * lax.rsqrt(jnp.mean(vc * vc, axis=-1, keepdims=True) + LN_EPS)
        vn = (vn * lng_ref[...] + lnb_ref[...]).astype(BF16)
        for g in range(SGU_GROUPS):
            w = jnp.where(causal, w_ref[g], 0.0).astype(BF16)
            mixed = jnp.dot(w, vn[:, g * gw:(g + 1) * gw], preferred_element_type=F32) + b_ref[g]
            u = u_ref[lo:lo + SGU_CHUNK, g * gw:(g + 1) * gw].astype(F32)
            o_ref[lo:lo + SGU_CHUNK, g * gw:(g + 1) * gw] = (u * mixed).astype(o_ref.dtype)


def _sgu(uv, ln_g, ln_b, w_s, b_s):
    t = uv.shape[0]
    b_bcast = jnp.broadcast_to(b_s[:, :, None], (SGU_GROUPS, SGU_CHUNK, SGU_WIDTH // SGU_GROUPS))
    full3 = lambda i: (0, 0, 0)
    return pl.pallas_call(
        _sgu_kernel,
        grid=(t // SGU_ROWS,),
        in_specs=[pl.BlockSpec((SGU_ROWS, SGU_WIDTH), lambda i: (i, 0)),
                  pl.BlockSpec((SGU_ROWS, SGU_WIDTH), lambda i: (i, 1)),
                  pl.BlockSpec((1, SGU_WIDTH), lambda i: (0, 0)),
                  pl.BlockSpec((1, SGU_WIDTH), lambda i: (0, 0)),
                  pl.BlockSpec(w_s.shape, full3),
                  pl.BlockSpec(b_bcast.shape, full3)],
        out_specs=pl.BlockSpec((SGU_ROWS, SGU_WIDTH), lambda i: (i, 0)),
        out_shape=jax.ShapeDtypeStruct((t, SGU_WIDTH), BF16),
        compiler_params=_params("arbitrary"),
        name="sgu",
    )(uv, uv, ln_g.reshape(1, -1), ln_b.reshape(1, -1), w_s, b_bcast)


def _merge_kernel(a_ref, b_ref, ga_ref, gb_ref, wa_ref, wb_ref, o_ref, wabf_ref, wbbf_ref):
    _cast_weight_once(wa_ref, wabf_ref)
    _cast_weight_once(wb_ref, wbbf_ref)
    ya = jnp.dot(a_ref[...], wabf_ref[...], preferred_element_type=F32)
    yb = jnp.dot(b_ref[...], wbbf_ref[...], preferred_element_type=F32)
    o_ref[...] = (ga_ref[...].astype(F32) * ya + gb_ref[...].astype(F32) * yb).astype(o_ref.dtype)


def _merge(sgu_act, attn_act, gates, w_proj_sgu, w_proj_attn):
    t, k = sgu_act.shape
    n = w_proj_sgu.shape[1]
    nb = n // PROJ_BN
    return pl.pallas_call(
        _merge_kernel,
        grid=(nb, t // PROJ_BM),
        in_specs=[pl.BlockSpec((PROJ_BM, k), lambda j, i: (i, 0)),
                  pl.BlockSpec((PROJ_BM, k), lambda j, i: (i, 0)),
                  pl.BlockSpec((PROJ_BM, PROJ_BN), lambda j, i: (i, j)),
                  pl.BlockSpec((PROJ_BM, PROJ_BN), lambda j, i: (i, j + nb)),
                  pl.BlockSpec((k, PROJ_BN), lambda j, i: (0, j)),
                  pl.BlockSpec((k, PROJ_BN), lambda j, i: (0, j))],
        out_specs=pl.BlockSpec((PROJ_BM, PROJ_BN), lambda j, i: (i, j)),
        out_shape=jax.ShapeDtypeStruct((t, n), BF16),
        scratch_shapes=[pltpu.VMEM((k, PROJ_BN), BF16), pltpu.VMEM((k, PROJ_BN), BF16)],
        compiler_params=_params("arbitrary", "arbitrary"),
        name="proj_merge",
    )(sgu_act, attn_act, gates, gates, w_proj_sgu, w_proj_attn)


def _plain_kernel(h_ref, w_ref, o_ref, wbf_ref):
    _cast_weight_once(w_ref, wbf_ref)
    o_ref[...] = jnp.dot(h_ref[...], wbf_ref[...], preferred_element_type=F32)


def _out_proj(merged, w_out):
    t, k = merged.shape
    n = w_out.shape[1]
    return pl.pallas_call(
        _plain_kernel,
        grid=(n // PROJ_BN, t // PROJ_BM),
        in_specs=[pl.BlockSpec((PROJ_BM, k), lambda j, i: (i, 0)),
                  pl.BlockSpec((k, PROJ_BN), lambda j, i: (0, j))],
        out_specs=pl.BlockSpec((PROJ_BM, PROJ_BN), lambda j, i: (i, j)),
        out_shape=jax.ShapeDtypeStruct((t, n), F32),
        scratch_shapes=[pltpu.VMEM((k, PROJ_BN), BF16)],
        compiler_params=_params("arbitrary", "arbitrary"),
        name="out_proj",
    )(merged, w_out)


RES_BM = 256


def _mid_kernel(x_ref, t_ref, gpost_ref, gt_ref, gpre_ref, sh_ref, sc_ref, x1_ref, h2_ref):
    x1 = x_ref[...] + gt_ref[...] * _rms(t_ref[...], gpost_ref[...])
    x1_ref[...] = x1
    h2_ref[...] = (_rms(x1, gpre_ref[...]) * (1.0 + sc_ref[...]) + sh_ref[...]).astype(h2_ref.dtype)


def _final_kernel(x_ref, f_ref, gpost_ref, gt_ref, o_ref):
    o_ref[...] = x_ref[...] + gt_ref[...] * _rms(f_ref[...], gpost_ref[...])


def _res_mod_spec(blocks_per_batch, m):
    return pl.BlockSpec((None, 1, D_MODEL),
                        lambda i: ((i // blocks_per_batch) * N_MOD + m, 0, 0))


def _mid_epilogue(x2, tproj, g_post, g_pre, mod3, seq):
    t, d = x2.shape
    bpb = seq // RES_BM
    row = pl.BlockSpec((RES_BM, d), lambda i: (i, 0))
    vec = pl.BlockSpec((1, d), lambda i: (0, 0))
    return pl.pallas_call(
        _mid_kernel,
        grid=(t // RES_BM,),
        in_specs=[row, row, vec, _res_mod_spec(bpb, 2), vec,
                  _res_mod_spec(bpb, 3), _res_mod_spec(bpb, 4)],
        out_specs=[row, row],
        out_shape=[jax.ShapeDtypeStruct((t, d), F32), jax.ShapeDtypeStruct((t, d), BF16)],
        compiler_params=_params("arbitrary"),
        name="mid_epilogue",
    )(x2, tproj, g_post.reshape(1, d), mod3, g_pre.reshape(1, d), mod3, mod3)


def _final_epilogue(x1, f, g_post, mod3, seq):
    t, d = x1.shape
    bpb = seq // RES_BM
    row = pl.BlockSpec((RES_BM, d), lambda i: (i, 0))
    vec = pl.BlockSpec((1, d), lambda i: (0, 0))
    return pl.pallas_call(
        _final_kernel,
        grid=(t // RES_BM,),
        in_specs=[row, row, vec, _res_mod_spec(bpb, 5)],
        out_specs=row,
        out_shape=jax.ShapeDtypeStruct((t, d), F32),
        compiler_params=_params("arbitrary"),
        name="final_epilogue",
    )(x1, f, g_post.reshape(1, d), mod3)


FFN_BM = 1024
FFN_BH = 256


def _ffn_kernel(h_ref, wg_ref, wu_ref, wd_ref, o_ref):
    @pl.when(pl.program_id(1) == 0)
    def _():
        o_ref[...] = jnp.zeros_like(o_ref)

    h = h_ref[...]
    a = jnp.dot(h, wg_ref[...], preferred_element_type=F32)
    b = jnp.dot(h, wu_ref[...], preferred_element_type=F32)
    g = (jax.nn.silu(a) * b).astype(BF16)
    o_ref[...] += jnp.dot(g, wd_ref[...], preferred_element_type=F32)


def _ffn(h2, wg, wu, wd):
    t, d = h2.shape
    hid = wg.shape[1]
    return pl.pallas_call(
        _ffn_kernel,
        grid=(t // FFN_BM, hid // FFN_BH),
        in_specs=[pl.BlockSpec((FFN_BM, d), lambda i, j: (i, 0), pipeline_mode=pl.Buffered(1)),
                  pl.BlockSpec((d, FFN_BH), lambda i, j: (0, j)),
                  pl.BlockSpec((d, FFN_BH), lambda i, j: (0, j)),
                  pl.BlockSpec((FFN_BH, d), lambda i, j: (j, 0))],
        out_specs=pl.BlockSpec((FFN_BM, d), lambda i, j: (i, 0)),
        out_shape=jax.ShapeDtypeStruct((t, d), F32),
        compiler_params=_params("arbitrary", "arbitrary"),
        name="ffn",
    )(h2, wg, wu, wd)


def _rope_tables(positions):
    half = ROT_DIM // 2
    inv_freq = ROPE_THETA ** (-jnp.arange(0, ROT_DIM, 2, dtype=F32) / ROT_DIM)
    ang = positions.reshape(-1).astype(F32)[:, None] * inv_freq
    cos, sin = jnp.cos(ang), jnp.sin(ang)
    t = ang.shape[0]
    ones = jnp.ones((t, HEAD_DIM - ROT_DIM), F32)
    zeros_h = jnp.zeros((t, half), F32)
    zeros_r = jnp.zeros((t, HEAD_DIM - ROT_DIM), F32)
    c = jnp.concatenate([cos, cos, ones], axis=1)
    s1 = jnp.concatenate([-sin, zeros_h, zeros_r], axis=1)
    s2 = jnp.concatenate([zeros_h, sin, zeros_r], axis=1)
    return c, s1, s2


def kernel(x, c, positions, w_ada, b_ada, g_pre_mix, w_in, attn_sinks, sgu_ln_g, sgu_ln_b, sgu_w, sgu_b, w_proj_sgu, w_proj_attn, w_out, g_post_mix, g_pre_ffn, w_gate, w_up, w_down, g_post_ffn):
    bsz, seq, d = x.shape
    depth = w_ada.shape[0]
    t = bsz * seq
    xcur = x.reshape(t, d)
    rope_c, rope_s1, rope_s2 = _rope_tables(positions)
    c_pad = jnp.pad(c, ((0, 8 - bsz), (0, 0)))
    for l in range(depth):
        mod = _ada_mod(c_pad, w_ada[l], b_ada[l])[:bsz]
        mod3 = mod.reshape(bsz * N_MOD, 1, d)

        h1 = _prenorm(xcur, g_pre_mix[l], mod3, seq, 0, 1)
        qkv = _in_proj(_qkv_kernel, h1, w_in[l], 0, QKV_WIDTH,
                       extra=(rope_c, rope_s1, rope_s2),
                       extra_scratch=(pltpu.VMEM((PROJ_BM, PROJ_BN), F32),), name="in_proj_qkv")
        uv = _in_proj(functools.partial(_act_kernel, _gelu), h1, w_in[l], QKV_WIDTH,
                      2 * SGU_WIDTH, name="in_proj_uv")
        gates = _in_proj(functools.partial(_act_kernel, jax.nn.sigmoid), h1, w_in[l],
                         QKV_WIDTH + 2 * SGU_WIDTH, 2 * d, name="in_proj_gates")

        attn_act = _attention(qkv, attn_sinks[l], seq)
        sgu_act = _sgu(uv, sgu_ln_g[l], sgu_ln_b[l], sgu_w[l], sgu_b[l])
        merged = _merge(sgu_act, attn_act, gates, w_proj_sgu[l], w_proj_attn[l])
        tproj = _out_proj(merged, w_out[l])
        x1, h2 = _mid_epilogue(xcur, tproj, g_post_mix[l], g_pre_ffn[l], mod3, seq)

        f = _ffn(h2, w_gate[l].astype(BF16), w_up[l].astype(BF16), w_down[l].astype(BF16))
        xcur = _final_epilogue(x1, f, g_post_ffn[l], mod3, seq)
    return xcur.reshape(bsz, seq, d)
```

```python
import functools

import jax
import jax.numpy as jnp
from jax import lax
from jax.experimental import pallas as pl
from jax.experimental.pallas import tpu as pltpu

D_MODEL = 4096
HEAD_DIM = 128
N_Q_HEADS = 16
N_KV_HEADS = 4
Q_PER_KV = 4
ATTN_WIDTH = N_Q_HEADS * HEAD_DIM
KV_WIDTH = N_KV_HEADS * HEAD_DIM
QKV_WIDTH = ATTN_WIDTH + 2 * KV_WIDTH
WINDOW = 128
BLOCK = 128
ROT_DIM = 32
ROPE_THETA = 500000.0
SGU_CHUNK = 128
SGU_GROUPS = 16
SGU_WIDTH = 2048
FFN_HIDDEN = 11008
N_MOD = 6
RMS_EPS = 1e-6
LN_EPS = 1e-5

VMEM_LIMIT_BYTES = 56 * 1024 * 1024

BF16 = jnp.bfloat16
F32 = jnp.float32


def _params(*semantics):
    return pltpu.CompilerParams(dimension_semantics=semantics,
                                vmem_limit_bytes=VMEM_LIMIT_BYTES)


def _rms(v, g):
    return v * lax.rsqrt(jnp.mean(v * v, axis=-1, keepdims=True) + RMS_EPS) * g


ADA_BN = 1024


def _ada_kernel(c_ref, w_ref, b_ref, o_ref):
    c_act = jax.nn.silu(c_ref[...]).astype(BF16)
    o_ref[...] = jnp.dot(c_act, w_ref[...].astype(BF16),
                         preferred_element_type=F32) + b_ref[...]


def _ada_mod(c_pad, w_ada, b_ada):
    rows, d = c_pad.shape
    n = w_ada.shape[1]
    return pl.pallas_call(
        _ada_kernel,
        grid=(n // ADA_BN,),
        in_specs=[pl.BlockSpec((rows, d), lambda j: (0, 0)),
                  pl.BlockSpec((d, ADA_BN), lambda j: (0, j)),
                  pl.BlockSpec((1, ADA_BN), lambda j: (0, j))],
        out_specs=pl.BlockSpec((rows, ADA_BN), lambda j: (0, j)),
        out_shape=jax.ShapeDtypeStruct((rows, n), F32),
        compiler_params=_params("arbitrary"),
        name="ada_mod",
    )(c_pad, w_ada, b_ada.reshape(1, n))


NORM_BM = 512


def _prenorm_kernel(x_ref, g_ref, sh_ref, sc_ref, o_ref):
    y = _rms(x_ref[...], g_ref[...])
    o_ref[...] = (y * (1.0 + sc_ref[...]) + sh_ref[...]).astype(o_ref.dtype)


def _mod_spec(blocks_per_batch, m):
    return pl.BlockSpec((None, 1, D_MODEL),
                        lambda i: ((i // blocks_per_batch) * N_MOD + m, 0, 0))


def _prenorm(x2, g, mod3, seq, m_shift, m_scale):
    t, d = x2.shape
    bpb = seq // NORM_BM
    return pl.pallas_call(
        _prenorm_kernel,
        grid=(t // NORM_BM,),
        in_specs=[pl.BlockSpec((NORM_BM, d), lambda i: (i, 0)),
                  pl.BlockSpec((1, d), lambda i: (0, 0)),
                  _mod_spec(bpb, m_shift), _mod_spec(bpb, m_scale)],
        out_specs=pl.BlockSpec((NORM_BM, d), lambda i: (i, 0)),
        out_shape=jax.ShapeDtypeStruct((t, d), BF16),
        compiler_params=_params("arbitrary"),
        name="prenorm",
    )(x2, g.reshape(1, d), mod3, mod3)


PROJ_BM = 1024
PROJ_BN = 512


def _cast_weight_once(w_ref, wbf_ref):
    @pl.when(pl.program_id(1) == 0)
    def _():
        wbf_ref[...] = w_ref[...].astype(BF16)


ROPE_ROWS = 128
CAST_ROWS_UP = 16
CAST_ROWS_DOWN = 128


def _rope_epilogue(z, o_ref, tables, scratch):
    c_ref, s1_ref, s2_ref = tables
    (z_ref,) = scratch
    z_ref[...] = z
    is_v = pl.program_id(0) >= (ATTN_WIDTH + KV_WIDTH) // PROJ_BN
    half = ROT_DIM // 2

    @pl.when(jnp.logical_not(is_v))
    def _():
        for r in range(0, PROJ_BM, ROPE_ROWS):
            c = c_ref[r:r + ROPE_ROWS, :]
            s1 = s1_ref[r:r + ROPE_ROWS, :]
            s2 = s2_ref[r:r + ROPE_ROWS, :]
            for hd in range(0, PROJ_BN, HEAD_DIM):
                zh = z_ref[r:r + ROPE_ROWS, hd:hd + HEAD_DIM]
                up = pltpu.roll(zh, HEAD_DIM - half, 1)
                down = pltpu.roll(zh, half, 1)
                o_ref[r:r + ROPE_ROWS, hd:hd + HEAD_DIM] = (
                    zh * c + up * s1 + down * s2).astype(o_ref.dtype)

    @pl.when(is_v)
    def _():
        o_ref[...] = z_ref[...].astype(o_ref.dtype)


def _act_epilogue(act, z, o_ref, tables, scratch):
    o_ref[...] = act(z).astype(o_ref.dtype)


def _gelu(z):
    return 0.5 * z * (1.0 + lax.erf(z * (0.5 ** 0.5)))


def _proj_kernel(epilogue, n_tables, n_casts, *refs):
    h_ref, w_ref = refs[:2]
    tables = refs[2:2 + n_tables]
    cast_in = refs[2 + n_tables:2 + n_tables + n_casts]
    o_ref = refs[2 + n_tables + n_casts]
    cast_out = refs[3 + n_tables + n_casts:3 + n_tables + 2 * n_casts]
    wbf_ref = refs[3 + n_tables + 2 * n_casts]
    scratch = refs[4 + n_tables + 2 * n_casts:]
    _cast_weight_once(w_ref, wbf_ref)
    for src, dst in zip(cast_in, cast_out):
        dst[...] = src[...].astype(BF16)
    z = jnp.dot(h_ref[...], wbf_ref[...], preferred_element_type=F32)
    epilogue(z, o_ref, tables, scratch)


def _in_proj(epilogue, h, w_in, col_off, width, tables=(), casts=(), extra_scratch=(), name=None):
    t, d = h.shape
    nj, ni = width // PROJ_BN, t // PROJ_BM
    off_blocks = col_off // PROJ_BN
    table_specs = [pl.BlockSpec((PROJ_BM, a.shape[1]), lambda j, i: (i, 0)) for a in tables]
    cast_specs = []
    for a, rows in casts:
        n_slices = a.shape[0] // rows
        assert n_slices * rows == a.shape[0] and n_slices <= nj * ni, (a.shape, rows)
        cast_specs.append(pl.BlockSpec(
            (rows, a.shape[1]),
            functools.partial(lambda j, i, last: (jnp.minimum(j * ni + i, last), 0), last=n_slices - 1)))
    casts = [a for a, _ in casts]
    return pl.pallas_call(
        functools.partial(_proj_kernel, epilogue, len(tables), len(casts)),
        grid=(nj, ni),
        in_specs=[pl.BlockSpec((PROJ_BM, d), lambda j, i: (i, 0)),
                  pl.BlockSpec((d, PROJ_BN), lambda j, i: (0, j + off_blocks))] + table_specs + cast_specs,
        out_specs=[pl.BlockSpec((PROJ_BM, PROJ_BN), lambda j, i: (i, j))] + cast_specs,
        out_shape=[jax.ShapeDtypeStruct((t, width), BF16)]
        + [jax.ShapeDtypeStruct(a.shape, BF16) for a in casts],
        scratch_shapes=[pltpu.VMEM((d, PROJ_BN), BF16), *extra_scratch],
        compiler_params=_params("arbitrary", "arbitrary"),
        name=name,
    )(h, w_in, *tables, *casts)


ATT_ROWS = 1024


def _attn_kernel(sink_ref, q_ref, kc_ref, kp_ref, vc_ref, vp_ref, o_ref, *, blocks_per_seq):
    r = pl.program_id(0)
    h = pl.program_id(1)
    seq_start = (r % blocks_per_seq) == 0
    rows = Q_PER_KV * BLOCK
    qi = lax.broadcasted_iota(jnp.int32, (rows, 2 * BLOCK), 0) % BLOCK
    ki = lax.broadcasted_iota(jnp.int32, (rows, 2 * BLOCK), 1)
    diff = qi + BLOCK - ki
    band = (diff >= 0) & (diff < WINDOW)
    band_first = band & ((ki >= BLOCK) | jnp.logical_not(seq_start))
    sink = jnp.concatenate(
        [jnp.full((BLOCK, 1), sink_ref[h * Q_PER_KV + g], F32) for g in range(Q_PER_KV)], axis=0)
    scale = HEAD_DIM ** -0.5
    for n in range(ATT_ROWS // BLOCK):
        lo = n * BLOCK
        q4 = jnp.concatenate(
            [q_ref[lo:lo + BLOCK, g * HEAD_DIM:(g + 1) * HEAD_DIM] for g in range(Q_PER_KV)], axis=0)
        if n == 0:
            kk = jnp.concatenate([kp_ref[...], kc_ref[0:BLOCK, :]], axis=0)
            vv = jnp.concatenate([vp_ref[...], vc_ref[0:BLOCK, :]], axis=0)
            valid = band_first
        else:
            kk = kc_ref[lo - BLOCK:lo + BLOCK, :]
            vv = vc_ref[lo - BLOCK:lo + BLOCK, :]
            valid = band
        s = lax.dot_general(q4, kk, (((1,), (1,)), ((), ())), preferred_element_type=F32) * scale
        s = jnp.where(valid, s, -jnp.inf)
        m = jnp.maximum(jnp.max(s, axis=-1, keepdims=True), sink)
        p = jnp.exp(s - m)
        denom = jnp.sum(p, axis=-1, keepdims=True) + jnp.exp(sink - m)
        o = jnp.dot(p.astype(BF16), vv, preferred_element_type=F32) / denom
        for g in range(Q_PER_KV):
            o_ref[lo:lo + BLOCK, g * HEAD_DIM:(g + 1) * HEAD_DIM] = (
                o[g * BLOCK:(g + 1) * BLOCK, :].astype(o_ref.dtype))


def _attention(qkv, sinks, seq):
    t = qkv.shape[0]
    bps = seq // ATT_ROWS
    qw = Q_PER_KV * HEAD_DIM
    k_col = ATTN_WIDTH // HEAD_DIM
    v_col = (ATTN_WIDTH + KV_WIDTH) // HEAD_DIM
    sub = ATT_ROWS // BLOCK

    def prev(r):
        return jnp.maximum(r * sub - 1, 0)

    return pl.pallas_call(
        functools.partial(_attn_kernel, blocks_per_seq=bps),
        grid=(t // ATT_ROWS, N_KV_HEADS),
        in_specs=[pl.BlockSpec(memory_space=pltpu.SMEM),
                  pl.BlockSpec((ATT_ROWS, qw), lambda r, h: (r, h)),
                  pl.BlockSpec((ATT_ROWS, HEAD_DIM), lambda r, h: (r, k_col + h)),
                  pl.BlockSpec((BLOCK, HEAD_DIM), lambda r, h: (prev(r), k_col + h)),
                  pl.BlockSpec((ATT_ROWS, HEAD_DIM), lambda r, h: (r, v_col + h)),
                  pl.BlockSpec((BLOCK, HEAD_DIM), lambda r, h: (prev(r), v_col + h))],
        out_specs=pl.BlockSpec((ATT_ROWS, qw), lambda r, h: (r, h)),
        out_shape=jax.ShapeDtypeStruct((t, ATTN_WIDTH), BF16),
        compiler_params=_params("arbitrary", "arbitrary"),
        name="swa_attention",
    )(sinks, qkv, qkv, qkv, qkv, qkv)


SGU_ROWS = 512


def _sgu_kernel(u_ref, v_ref, lng_ref, lnb_ref, w_ref, b_ref, o_ref):
    ti = lax.broadcasted_iota(jnp.int32, (SGU_CHUNK, SGU_CHUNK), 0)
    si = lax.broadcasted_iota(jnp.int32, (SGU_CHUNK, SGU_CHUNK), 1)
    causal = si <= ti
    gw = SGU_WIDTH // SGU_GROUPS
    for c in range(SGU_ROWS // SGU_CHUNK):
        lo = c * SGU_CHUNK
        v = v_ref[lo:lo + SGU_CHUNK, :].astype(F32)
        mu = jnp.mean(v, axis=-1, keepdims=True)
        vc = v - mu
        vn = vc * lax.rsqrt(jnp.mean(vc * vc, axis=-1, keepdims=True) + LN_EPS)
        vn = (vn * lng_ref[...] + lnb_ref[...]).astype(BF16)
        for g in range(SGU_GROUPS):
            w = jnp.where(causal, w_ref[g], 0.0).astype(BF16)
            mixed = jnp.dot(w, vn[:, g * gw:(g + 1) * gw], preferred_element_type=F32) + b_ref[g]
            u = u_ref[lo:lo + SGU_CHUNK, g * gw:(g + 1) * gw].astype(F32)
            o_ref[lo:lo + SGU_CHUNK, g * gw:(g + 1) * gw] = (u * mixed).astype(o_ref.dtype)


def _sgu(uv, ln_g, ln_b, w_s, b_s):
    t = uv.shape[0]
    b_bcast = jnp.broadcast_to(b_s[:, :, None], (SGU_GROUPS, SGU_CHUNK, SGU_WIDTH // SGU_GROUPS))
    full3 = lambda i: (0, 0, 0)
    return pl.pallas_call(
        _sgu_kernel,
        grid=(t // SGU_ROWS,),
        in_specs=[pl.BlockSpec((SGU_ROWS, SGU_WIDTH), lambda i: (i, 0)),
                  pl.BlockSpec((SGU_ROWS, SGU_WIDTH), lambda i: (i, 1)),
                  pl.BlockSpec((1, SGU_WIDTH), lambda i: (0, 0)),
                  pl.BlockSpec((1, SGU_WIDTH), lambda i: (0, 0)),
                  pl.BlockSpec(w_s.shape, full3),
                  pl.BlockSpec(b_bcast.shape, full3)],
        out_specs=pl.BlockSpec((SGU_ROWS, SGU_WIDTH), lambda i: (i, 0)),
        out_shape=jax.ShapeDtypeStruct((t, SGU_WIDTH), BF16),
        compiler_params=_params("arbitrary"),
        name="sgu",
    )(uv, uv, ln_g.reshape(1, -1), ln_b.reshape(1, -1), w_s, b_bcast)


def _merge_kernel(a_ref, b_ref, ga_ref, gb_ref, wa_ref, wb_ref, o_ref, wabf_ref, wbbf_ref):
    _cast_weight_once(wa_ref, wabf_ref)
    _cast_weight_once(wb_ref, wbbf_ref)
    ya = jnp.dot(a_ref[...], wabf_ref[...], preferred_element_type=F32)
    yb = jnp.dot(b_ref[...], wbbf_ref[...], preferred_element_type=F32)
    o_ref[...] = (ga_ref[...].astype(F32) * ya + gb_ref[...].astype(F32) * yb).astype(o_ref.dtype)


def _merge(sgu_act, attn_act, gates, w_proj_sgu, w_proj_attn):
    t, k = sgu_act.shape
    n = w_proj_sgu.shape[1]
    nb = n // PROJ_BN
    return pl.pallas_call(
        _merge_kernel,
        grid=(nb, t // PROJ_BM),
        in_specs=[pl.BlockSpec((PROJ_BM, k), lambda j, i: (i, 0)),
                  pl.BlockSpec((PROJ_BM, k), lambda j, i: (i, 0)),
                  pl.BlockSpec((PROJ_BM, PROJ_BN), lambda j, i: (i, j)),
                  pl.BlockSpec((PROJ_BM, PROJ_BN), lambda j, i: (i, j + nb)),
                  pl.BlockSpec((k, PROJ_BN), lambda j, i: (0, j)),
                  pl.BlockSpec((k, PROJ_BN), lambda j, i: (0, j))],
        out_specs=pl.BlockSpec((PROJ_BM, PROJ_BN), lambda j, i: (i, j)),
        out_shape=jax.ShapeDtypeStruct((t, n), BF16),
        scratch_shapes=[pltpu.VMEM((k, PROJ_BN), BF16), pltpu.VMEM((k, PROJ_BN), BF16)],
        compiler_params=_params("arbitrary", "arbitrary"),
        name="proj_merge",
    )(sgu_act, attn_act, gates, gates, w_proj_sgu, w_proj_attn)


def _plain_kernel(h_ref, w_ref, o_ref, wbf_ref):
    _cast_weight_once(w_ref, wbf_ref)
    o_ref[...] = jnp.dot(h_ref[...], wbf_ref[...], preferred_element_type=F32)


def _out_proj(merged, w_out):
    t, k = merged.shape
    n = w_out.shape[1]
    return pl.pallas_call(
        _plain_kernel,
        grid=(n // PROJ_BN, t // PROJ_BM),
        in_specs=[pl.BlockSpec((PROJ_BM, k), lambda j, i: (i, 0)),
                  pl.BlockSpec((k, PROJ_BN), lambda j, i: (0, j))],
        out_specs=pl.BlockSpec((PROJ_BM, PROJ_BN), lambda j, i: (i, j)),
        out_shape=jax.ShapeDtypeStruct((t, n), F32),
        scratch_shapes=[pltpu.VMEM((k, PROJ_BN), BF16)],
        compiler_params=_params("arbitrary", "arbitrary"),
        name="out_proj",
    )(merged, w_out)


RES_BM = 256


def _mid_kernel(x_ref, t_ref, gpost_ref, gt_ref, gpre_ref, sh_ref, sc_ref, x1_ref, h2_ref):
    x1 = x_ref[...] + gt_ref[...] * _rms(t_ref[...], gpost_ref[...])
    x1_ref[...] = x1
    h2_ref[...] = (_rms(x1, gpre_ref[...]) * (1.0 + sc_ref[...]) + sh_ref[...]).astype(h2_ref.dtype)


def _res_mod_spec(blocks_per_batch, m):
    return pl.BlockSpec((None, 1, D_MODEL),
                        lambda i: ((i // blocks_per_batch) * N_MOD + m, 0, 0))


def _mid_epilogue(x2, tproj, g_post, g_pre, mod3, seq):
    t, d = x2.shape
    bpb = seq // RES_BM
    row = pl.BlockSpec((RES_BM, d), lambda i: (i, 0))
    vec = pl.BlockSpec((1, d), lambda i: (0, 0))
    return pl.pallas_call(
        _mid_kernel,
        grid=(t // RES_BM,),
        in_specs=[row, row, vec, _res_mod_spec(bpb, 2), vec,
                  _res_mod_spec(bpb, 3), _res_mod_spec(bpb, 4)],
        out_specs=[row, row],
        out_shape=[jax.ShapeDtypeStruct((t, d), F32), jax.ShapeDtypeStruct((t, d), BF16)],
        compiler_params=_params("arbitrary"),
        name="mid_epilogue",
    )(x2, tproj, g_post.reshape(1, d), mod3, g_pre.reshape(1, d), mod3, mod3)


FFN_BM = 1024
FFN_BH = 256
EPI_ROWS = 32
EPI_SLABS = FFN_BM // EPI_ROWS


def _ffn_kernel(ni, h_ref, wg_ref, wu_ref, wd_ref, x1_ref, gpost_ref, gt_ref, o_ref,
                acc0_ref, acc1_ref):
    i = pl.program_id(0)
    j = pl.program_id(1)
    accs = (acc0_ref, acc1_ref)
    odd = (i % 2) == 1

    def accumulate(acc_ref):
        h = h_ref[...]
        a = jnp.dot(h, wg_ref[...], preferred_element_type=F32)
        b = jnp.dot(h, wu_ref[...], preferred_element_type=F32)
        g = (jax.nn.silu(a) * b).astype(BF16)
        acc_ref[...] += jnp.dot(g, wd_ref[...], preferred_element_type=F32)

    def write_rows(acc_ref):
        r = pl.multiple_of(jnp.minimum(j, EPI_SLABS - 1) * EPI_ROWS, EPI_ROWS)
        f = acc_ref[pl.ds(r, EPI_ROWS), :]
        o_ref[...] = x1_ref[...] + gt_ref[...] * _rms(f, gpost_ref[...])

    for par in (0, 1):
        is_par = odd if par else jnp.logical_not(odd)

        @pl.when((j == 0) & (i < ni) & is_par)
        def _():
            accs[par][...] = jnp.zeros(accs[par].shape, F32)

    @pl.when(i == 0)
    def _():
        accumulate(accs[0])

    for par in (0, 1):
        is_par = odd if par else jnp.logical_not(odd)

        @pl.when((i > 0) & (i < ni) & is_par)
        def _():
            write_rows(accs[1 - par])
            accumulate(accs[par])

    @pl.when(i == ni)
    def _():
        write_rows(accs[(ni - 1) % 2])


def _ffn(h2, wg, wu, wd, x1, g_post, mod3, seq):
    t, d = h2.shape
    hid = wg.shape[1]
    ni, nj = t // FFN_BM, hid // FFN_BH
    bpb = seq // FFN_BM

    def row_block(i):
        return jnp.minimum(i, ni - 1)

    def hid_block(i, j):
        return jnp.where(i == ni, nj - 1, j)

    def slab(i, j):
        return (jnp.where(i == 0, 0, (i - 1) * EPI_SLABS + jnp.minimum(j, EPI_SLABS - 1)), 0)

    def gate(i, j):
        return ((jnp.maximum(i - 1, 0) // bpb) * N_MOD + 5, 0, 0)

    return pl.pallas_call(
        functools.partial(_ffn_kernel, ni),
        grid=(ni + 1, nj),
        in_specs=[pl.BlockSpec((FFN_BM, d), lambda i, j: (row_block(i), 0),
                               pipeline_mode=pl.Buffered(1)),
                  pl.BlockSpec((d, FFN_BH), lambda i, j: (0, hid_block(i, j))),
                  pl.BlockSpec((d, FFN_BH), lambda i, j: (0, hid_block(i, j))),
                  pl.BlockSpec((FFN_BH, d), lambda i, j: (hid_block(i, j), 0)),
                  pl.BlockSpec((EPI_ROWS, d), slab),
                  pl.BlockSpec((1, d), lambda i, j: (0, 0)),
                  pl.BlockSpec((None, 1, d), gate)],
        out_specs=pl.BlockSpec((EPI_ROWS, d), slab),
        out_shape=jax.ShapeDtypeStruct((t, d), F32),
        scratch_shapes=[pltpu.VMEM((FFN_BM, d), F32), pltpu.VMEM((FFN_BM, d), F32)],
        compiler_params=_params("arbitrary", "arbitrary"),
        name="ffn",
    )(h2, wg, wu, wd, x1, g_post.reshape(1, d), mod3)


def _rope_tables(positions):
    half = ROT_DIM // 2
    inv_freq = ROPE_THETA ** (-jnp.arange(0, ROT_DIM, 2, dtype=F32) / ROT_DIM)
    ang = positions.reshape(-1).astype(F32)[:, None] * inv_freq
    cos, sin = jnp.cos(ang), jnp.sin(ang)
    t = ang.shape[0]
    ones = jnp.ones((t, HEAD_DIM - ROT_DIM), F32)
    zeros_h = jnp.zeros((t, half), F32)
    zeros_r = jnp.zeros((t, HEAD_DIM - ROT_DIM), F32)
    c = jnp.concatenate([cos, cos, ones], axis=1)
    s1 = jnp.concatenate([-sin, zeros_h, zeros_r], axis=1)
    s2 = jnp.concatenate([zeros_h, sin, zeros_r], axis=1)
    return c, s1, s2


def kernel(x, c, positions, w_ada, b_ada, g_pre_mix, w_in, attn_sinks, sgu_ln_g, sgu_ln_b, sgu_w, sgu_b, w_proj_sgu, w_proj_attn, w_out, g_post_mix, g_pre_ffn, w_gate, w_up, w_down, g_post_ffn):
    bsz, seq, d = x.shape
    depth = w_ada.shape[0]
    t = bsz * seq
    xcur = x.reshape(t, d)
    rope_c, rope_s1, rope_s2 = _rope_tables(positions)
    c_pad = jnp.pad(c, ((0, 8 - bsz), (0, 0)))
    for l in range(depth):
        mod = _ada_mod(c_pad, w_ada[l], b_ada[l])[:bsz]
        mod3 = mod.reshape(bsz * N_MOD, 1, d)

        h1 = _prenorm(xcur, g_pre_mix[l], mod3, seq, 0, 1)
        (qkv,) = _in_proj(_rope_epilogue, h1, w_in[l], 0, QKV_WIDTH,
                          tables=(rope_c, rope_s1, rope_s2),
                          extra_scratch=(pltpu.VMEM((PROJ_BM, PROJ_BN), F32),),
                          name="in_proj_qkv")
        uv, wd_bf = _in_proj(functools.partial(_act_epilogue, _gelu), h1, w_in[l], QKV_WIDTH,
                             2 * SGU_WIDTH, casts=((w_down[l], CAST_ROWS_DOWN),),
                             name="in_proj_uv")
        gates, wg_bf, wu_bf = _in_proj(functools.partial(_act_epilogue, jax.nn.sigmoid), h1,
                                       w_in[l], QKV_WIDTH + 2 * SGU_WIDTH, 2 * d,
                                       casts=((w_gate[l], CAST_ROWS_UP), (w_up[l], CAST_ROWS_UP)),
                                       name="in_proj_gates")

        attn_act = _attention(qkv, attn_sinks[l], seq)
        sgu_act = _sgu(uv, sgu_ln_g[l], sgu_ln_b[l], sgu_w[l], sgu_b[l])
        merged = _merge(sgu_act, attn_act, gates, w_proj_sgu[l], w_proj_attn[l])
        tproj = _out_proj(merged, w_out[l])
        x1, h2 = _mid_epilogue(xcur, tproj, g_post_mix[l], g_pre_ffn[l], mod3, seq)

        xcur = _ffn(h2, wg_bf, wu_bf, wd_bf, x1, g_post_ffn[l], mod3, seq)
    return xcur.reshape(bsz, seq, d)
```

```python
import functools

import jax
import jax.numpy as jnp
from jax import lax
from jax.experimental import pallas as pl
from jax.experimental.pallas import tpu as pltpu

D_MODEL = 4096
HEAD_DIM = 128
N_Q_HEADS = 16
N_KV_HEADS = 4
Q_PER_KV = 4
ATTN_WIDTH = N_Q_HEADS * HEAD_DIM
KV_WIDTH = N_KV_HEADS * HEAD_DIM
QKV_WIDTH = ATTN_WIDTH + 2 * KV_WIDTH
WINDOW = 128
BLOCK = 128
ROT_DIM = 32
ROPE_THETA = 500000.0
SGU_CHUNK = 128
SGU_GROUPS = 16
SGU_WIDTH = 2048
FFN_HIDDEN = 11008
N_MOD = 6
RMS_EPS = 1e-6
LN_EPS = 1e-5

VMEM_LIMIT_BYTES = 56 * 1024 * 1024

BF16 = jnp.bfloat16
F32 = jnp.float32


def _params(*semantics):
    return pltpu.CompilerParams(dimension_semantics=semantics,
                                vmem_limit_bytes=VMEM_LIMIT_BYTES)


def _rms(v, g):
    return v * lax.rsqrt(jnp.mean(v * v, axis=-1, keepdims=True) + RMS_EPS) * g


ADA_BN = 1024


def _ada_kernel(c_ref, w_ref, b_ref, o_ref):
    c_act = jax.nn.silu(c_ref[...]).astype(BF16)
    o_ref[...] = jnp.dot(c_act, w_ref[...].astype(BF16),
                         preferred_element_type=F32) + b_ref[...]


def _ada_mod(c_pad, w_ada, b_ada):
    rows, d = c_pad.shape
    n = w_ada.shape[1]
    return pl.pallas_call(
        _ada_kernel,
        grid=(n // ADA_BN,),
        in_specs=[pl.BlockSpec((rows, d), lambda j: (0, 0)),
                  pl.BlockSpec((d, ADA_BN), lambda j: (0, j)),
                  pl.BlockSpec((1, ADA_BN), lambda j: (0, j))],
        out_specs=pl.BlockSpec((rows, ADA_BN), lambda j: (0, j)),
        out_shape=jax.ShapeDtypeStruct((rows, n), F32),
        compiler_params=_params("arbitrary"),
        name="ada_mod",
    )(c_pad, w_ada, b_ada.reshape(1, n))


NORM_BM = 512


def _prenorm_kernel(x_ref, g_ref, sh_ref, sc_ref, o_ref):
    y = _rms(x_ref[...], g_ref[...])
    o_ref[...] = (y * (1.0 + sc_ref[...]) + sh_ref[...]).astype(o_ref.dtype)


def _mod_spec(blocks_per_batch, m):
    return pl.BlockSpec((None, 1, D_MODEL),
                        lambda i: ((i // blocks_per_batch) * N_MOD + m, 0, 0))


def _prenorm(x2, g, mod3, seq, m_shift, m_scale):
    t, d = x2.shape
    bpb = seq // NORM_BM
    return pl.pallas_call(
        _prenorm_kernel,
        grid=(t // NORM_BM,),
        in_specs=[pl.BlockSpec((NORM_BM, d), lambda i: (i, 0)),
                  pl.BlockSpec((1, d), lambda i: (0, 0)),
                  _mod_spec(bpb, m_shift), _mod_spec(bpb, m_scale)],
        out_specs=pl.BlockSpec((NORM_BM, d), lambda i: (i, 0)),
        out_shape=jax.ShapeDtypeStruct((t, d), BF16),
        compiler_params=_params("arbitrary"),
        name="prenorm",
    )(x2, g.reshape(1, d), mod3, mod3)


PROJ_BM = 1024
PROJ_BN = 512


def _cast_weight_once(w_ref, wbf_ref):
    @pl.when(pl.program_id(1) == 0)
    def _():
        wbf_ref[...] = w_ref[...].astype(BF16)


ROPE_ROWS = 128
CAST_ROWS_UP = 16
CAST_ROWS_DOWN = 128


def _rope_epilogue(z, o_ref, tables, scratch):
    c_ref, s1_ref, s2_ref = tables
    (z_ref,) = scratch
    z_ref[...] = z
    is_v = pl.program_id(0) >= (ATTN_WIDTH + KV_WIDTH) // PROJ_BN
    half = ROT_DIM // 2

    @pl.when(jnp.logical_not(is_v))
    def _():
        for r in range(0, PROJ_BM, ROPE_ROWS):
            c = c_ref[r:r + ROPE_ROWS, :]
            s1 = s1_ref[r:r + ROPE_ROWS, :]
            s2 = s2_ref[r:r + ROPE_ROWS, :]
            for hd in range(0, PROJ_BN, HEAD_DIM):
                zh = z_ref[r:r + ROPE_ROWS, hd:hd + HEAD_DIM]
                up = pltpu.roll(zh, HEAD_DIM - half, 1)
                down = pltpu.roll(zh, half, 1)
                o_ref[r:r + ROPE_ROWS, hd:hd + HEAD_DIM] = (
                    zh * c + up * s1 + down * s2).astype(o_ref.dtype)

    @pl.when(is_v)
    def _():
        o_ref[...] = z_ref[...].astype(o_ref.dtype)


def _act_epilogue(act, z, o_ref, tables, scratch):
    o_ref[...] = act(z).astype(o_ref.dtype)


def _gelu(z):
    return 0.5 * z * (1.0 + lax.erf(z * (0.5 ** 0.5)))


def _proj_kernel(epilogue, n_tables, n_casts, *refs):
    h_ref, w_ref = refs[:2]
    tables = refs[2:2 + n_tables]
    cast_in = refs[2 + n_tables:2 + n_tables + n_casts]
    o_ref = refs[2 + n_tables + n_casts]
    cast_out = refs[3 + n_tables + n_casts:3 + n_tables + 2 * n_casts]
    wbf_ref = refs[3 + n_tables + 2 * n_casts]
    scratch = refs[4 + n_tables + 2 * n_casts:]
    _cast_weight_once(w_ref, wbf_ref)
    for src, dst in zip(cast_in, cast_out):
        dst[...] = src[...].astype(BF16)
    z = jnp.dot(h_ref[...], wbf_ref[...], preferred_element_type=F32)
    epilogue(z, o_ref, tables, scratch)


def _in_proj(epilogue, h, w_in, col_off, width, tables=(), casts=(), extra_scratch=(), name=None):
    t, d = h.shape
    nj, ni = width // PROJ_BN, t // PROJ_BM
    off_blocks = col_off // PROJ_BN
    table_specs = [pl.BlockSpec((PROJ_BM, a.shape[1]), lambda j, i: (i, 0)) for a in tables]
    cast_specs = []
    for a, rows in casts:
        n_slices = a.shape[0] // rows
        assert n_slices * rows == a.shape[0] and n_slices <= nj * ni, (a.shape, rows)
        cast_specs.append(pl.BlockSpec(
            (rows, a.shape[1]),
            functools.partial(lambda j, i, last: (jnp.minimum(j * ni + i, last), 0), last=n_slices - 1)))
    casts = [a for a, _ in casts]
    return pl.pallas_call(
        functools.partial(_proj_kernel, epilogue, len(tables), len(casts)),
        grid=(nj, ni),
        in_specs=[pl.BlockSpec((PROJ_BM, d), lambda j, i: (i, 0)),
                  pl.BlockSpec((d, PROJ_BN), lambda j, i: (0, j + off_blocks))] + table_specs + cast_specs,
        out_specs=[pl.BlockSpec((PROJ_BM, PROJ_BN), lambda j, i: (i, j))] + cast_specs,
        out_shape=[jax.ShapeDtypeStruct((t, width), BF16)]
        + [jax.ShapeDtypeStruct(a.shape, BF16) for a in casts],
        scratch_shapes=[pltpu.VMEM((d, PROJ_BN), BF16), *extra_scratch],
        compiler_params=_params("arbitrary", "arbitrary"),
        name=name,
    )(h, w_in, *tables, *casts)


ATT_ROWS = 1024


def _attn_kernel(sink_ref, q_ref, kc_ref, kp_ref, vc_ref, vp_ref, o_ref, *, blocks_per_seq):
    r = pl.program_id(0)
    h = pl.program_id(1)
    seq_start = (r % blocks_per_seq) == 0
    qi = lax.broadcasted_iota(jnp.int32, (BLOCK, 2 * BLOCK), 0)
    ki = lax.broadcasted_iota(jnp.int32, (BLOCK, 2 * BLOCK), 1)
    diff = qi + BLOCK - ki
    band = (diff >= 0) & (diff < WINDOW)
    band_first = band & ((ki >= BLOCK) | jnp.logical_not(seq_start))
    sink_col = ki == 0
    v_row = lax.broadcasted_iota(jnp.int32, (2 * BLOCK, HEAD_DIM), 0)
    ones = jnp.ones((2 * BLOCK, HEAD_DIM), BF16)
    scale = HEAD_DIM ** -0.5
    for n in range(ATT_ROWS // BLOCK):
        lo = n * BLOCK
        q4 = jnp.concatenate(
            [q_ref[lo:lo + BLOCK, g * HEAD_DIM:(g + 1) * HEAD_DIM] for g in range(Q_PER_KV)], axis=0)
        if n == 0:
            kk = jnp.concatenate([kp_ref[...], kc_ref[0:BLOCK, :]], axis=0)
            vv = jnp.concatenate([vp_ref[...], vc_ref[0:BLOCK, :]], axis=0)
            valid = band_first
        else:
            kk = kc_ref[lo - BLOCK:lo + BLOCK, :]
            vv = vc_ref[lo - BLOCK:lo + BLOCK, :]
            valid = band
        v_ones = jnp.concatenate([jnp.where(v_row == 0, jnp.zeros_like(vv), vv), ones], axis=1)
        s = lax.dot_general(q4, kk, (((1,), (1,)), ((), ())), preferred_element_type=F32)
        p = []
        for g in range(Q_PER_KV):
            sg = s[g * BLOCK:(g + 1) * BLOCK, :] * scale
            sg = jnp.where(sink_col, sink_ref[h * Q_PER_KV + g], jnp.where(valid, sg, -jnp.inf))
            m = jnp.max(sg, axis=-1, keepdims=True)
            p.append(jnp.exp(sg - m).astype(BF16))
        o = jnp.dot(jnp.concatenate(p, axis=0), v_ones, preferred_element_type=F32)
        o = o[:, :HEAD_DIM] / o[:, HEAD_DIM:]
        for g in range(Q_PER_KV):
            o_ref[lo:lo + BLOCK, g * HEAD_DIM:(g + 1) * HEAD_DIM] = (
                o[g * BLOCK:(g + 1) * BLOCK, :].astype(o_ref.dtype))


def _attention(qkv, sinks, seq):
    t = qkv.shape[0]
    bps = seq // ATT_ROWS
    qw = Q_PER_KV * HEAD_DIM
    k_col = ATTN_WIDTH // HEAD_DIM
    v_col = (ATTN_WIDTH + KV_WIDTH) // HEAD_DIM
    sub = ATT_ROWS // BLOCK

    def prev(r):
        return jnp.maximum(r * sub - 1, 0)

    return pl.pallas_call(
        functools.partial(_attn_kernel, blocks_per_seq=bps),
        grid=(t // ATT_ROWS, N_KV_HEADS),
        in_specs=[pl.BlockSpec(memory_space=pltpu.SMEM),
                  pl.BlockSpec((ATT_ROWS, qw), lambda r, h: (r, h)),
                  pl.BlockSpec((ATT_ROWS, HEAD_DIM), lambda r, h: (r, k_col + h)),
                  pl.BlockSpec((BLOCK, HEAD_DIM), lambda r, h: (prev(r), k_col + h)),
                  pl.BlockSpec((ATT_ROWS, HEAD_DIM), lambda r, h: (r, v_col + h)),
                  pl.BlockSpec((BLOCK, HEAD_DIM), lambda r, h: (prev(r), v_col + h))],
        out_specs=pl.BlockSpec((ATT_ROWS, qw), lambda r, h: (r, h)),
        out_shape=jax.ShapeDtypeStruct((t, ATTN_WIDTH), BF16),
        compiler_params=_params("arbitrary", "arbitrary"),
        name="swa_attention",
    )(sinks, qkv, qkv, qkv, qkv, qkv)


SGU_ROWS = 512


def _sgu_kernel(u_ref, v_ref, lng_ref, lnb_ref, w_ref, b_ref, o_ref):
    ti = lax.broadcasted_iota(jnp.int32, (SGU_CHUNK, SGU_CHUNK), 0)
    si = lax.broadcasted_iota(jnp.int32, (SGU_CHUNK, SGU_CHUNK), 1)
    causal = si <= ti
    gw = SGU_WIDTH // SGU_GROUPS
    for c in range(SGU_ROWS // SGU_CHUNK):
        lo = c * SGU_CHUNK
        v = v_ref[lo:lo + SGU_CHUNK, :].astype(F32)
        mu = jnp.mean(v, axis=-1, keepdims=True)
        vc = v - mu
        vn = vc * lax.rsqrt(jnp.mean(vc * vc, axis=-1, keepdims=True) + LN_EPS)
        vn = (vn * lng_ref[...] + lnb_ref[...]).astype(BF16)
        for g in range(SGU_GROUPS):
            w = jnp.where(causal, w_ref[g], 0.0).astype(BF16)
            mixed = jnp.dot(w, vn[:, g * gw:(g + 1) * gw], preferred_element_type=F32) + b_ref[g]
            u = u_ref[lo:lo + SGU_CHUNK, g * gw:(g + 1) * gw].astype(F32)
            o_ref[lo:lo + SGU_CHUNK, g * gw:(g + 1) * gw] = (u * mixed).astype(o_ref.dtype)


def _sgu(uv, ln_g, ln_b, w_s, b_s):
    t = uv.shape[0]
    b_bcast = jnp.broadcast_to(b_s[:, :, None], (SGU_GROUPS, SGU_CHUNK, SGU_WIDTH // SGU_GROUPS))
    full3 = lambda i: (0, 0, 0)
    return pl.pallas_call(
        _sgu_kernel,
        grid=(t // SGU_ROWS,),
        in_specs=[pl.BlockSpec((SGU_ROWS, SGU_WIDTH), lambda i: (i, 0)),
                  pl.BlockSpec((SGU_ROWS, SGU_WIDTH), lambda i: (i, 1)),
                  pl.BlockSpec((1, SGU_WIDTH), lambda i: (0, 0)),
                  pl.BlockSpec((1, SGU_WIDTH), lambda i: (0, 0)),
                  pl.BlockSpec(w_s.shape, full3),
                  pl.BlockSpec(b_bcast.shape, full3)],
        out_specs=pl.BlockSpec((SGU_ROWS, SGU_WIDTH), lambda i: (i, 0)),
        out_shape=jax.ShapeDtypeStruct((t, SGU_WIDTH), BF16),
        compiler_params=_params("arbitrary"),
        name="sgu",
    )(uv, uv, ln_g.reshape(1, -1), ln_b.reshape(1, -1), w_s, b_bcast)


def _merge_kernel(a_ref, b_ref, ga_ref, gb_ref, wa_ref, wb_ref, o_ref, wabf_ref, wbbf_ref):
    _cast_weight_once(wa_ref, wabf_ref)
    _cast_weight_once(wb_ref, wbbf_ref)
    ya = jnp.dot(a_ref[...], wabf_ref[...], preferred_element_type=F32)
    yb = jnp.dot(b_ref[...], wbbf_ref[...], preferred_element_type=F32)
    ga = jax.nn.sigmoid(ga_ref[...].astype(F32))
    gb = jax.nn.sigmoid(gb_ref[...].astype(F32))
    o_ref[...] = (ga * ya + gb * yb).astype(o_ref.dtype)


def _merge(sgu_act, attn_act, gates, w_proj_sgu, w_proj_attn):
    t, k = sgu_act.shape
    n = w_proj_sgu.shape[1]
    nb = n // PROJ_BN
    return pl.pallas_call(
        _merge_kernel,
        grid=(nb, t // PROJ_BM),
        in_specs=[pl.BlockSpec((PROJ_BM, k), lambda j, i: (i, 0)),
                  pl.BlockSpec((PROJ_BM, k), lambda j, i: (i, 0)),
                  pl.BlockSpec((PROJ_BM, PROJ_BN), lambda j, i: (i, j)),
                  pl.BlockSpec((PROJ_BM, PROJ_BN), lambda j, i: (i, j + nb)),
                  pl.BlockSpec((k, PROJ_BN), lambda j, i: (0, j)),
                  pl.BlockSpec((k, PROJ_BN), lambda j, i: (0, j))],
        out_specs=pl.BlockSpec((PROJ_BM, PROJ_BN), lambda j, i: (i, j)),
        out_shape=jax.ShapeDtypeStruct((t, n), BF16),
        scratch_shapes=[pltpu.VMEM((k, PROJ_BN), BF16), pltpu.VMEM((k, PROJ_BN), BF16)],
        compiler_params=_params("arbitrary", "arbitrary"),
        name="proj_merge",
    )(sgu_act, attn_act, gates, gates, w_proj_sgu, w_proj_attn)


def _plain_kernel(h_ref, w_ref, o_ref, wbf_ref):
    _cast_weight_once(w_ref, wbf_ref)
    o_ref[...] = jnp.dot(h_ref[...], wbf_ref[...], preferred_element_type=F32)


def _out_proj(merged, w_out):
    t, k = merged.shape
    n = w_out.shape[1]
    return pl.pallas_call(
        _plain_kernel,
        grid=(n // PROJ_BN, t // PROJ_BM),
        in_specs=[pl.BlockSpec((PROJ_BM, k), lambda j, i: (i, 0)),
                  pl.BlockSpec((k, PROJ_BN), lambda j, i: (0, j))],
        out_specs=pl.BlockSpec((PROJ_BM, PROJ_BN), lambda j, i: (i, j)),
        out_shape=jax.ShapeDtypeStruct((t, n), F32),
        scratch_shapes=[pltpu.VMEM((k, PROJ_BN), BF16)],
        compiler_params=_params("arbitrary", "arbitrary"),
        name="out_proj",
    )(merged, w_out)


RES_BM = 256


def _mid_kernel(x_ref, t_ref, gpost_ref, gt_ref, gpre_ref, sh_ref, sc_ref, x1_ref, h2_ref):
    x1 = x_ref[...] + gt_ref[...] * _rms(t_ref[...], gpost_ref[...])
    x1_ref[...] = x1
    h2_ref[...] = (_rms(x1, gpre_ref[...]) * (1.0 + sc_ref[...]) + sh_ref[...]).astype(h2_ref.dtype)


def _res_mod_spec(blocks_per_batch, m):
    return pl.BlockSpec((None, 1, D_MODEL),
                        lambda i: ((i // blocks_per_batch) * N_MOD + m, 0, 0))


def _mid_epilogue(x2, tproj, g_post, g_pre, mod3, seq):
    t, d = x2.shape
    bpb = seq // RES_BM
    row = pl.BlockSpec((RES_BM, d), lambda i: (i, 0))
    vec = pl.BlockSpec((1, d), lambda i: (0, 0))
    return pl.pallas_call(
        _mid_kernel,
        grid=(t // RES_BM,),
        in_specs=[row, row, vec, _res_mod_spec(bpb, 2), vec,
                  _res_mod_spec(bpb, 3), _res_mod_spec(bpb, 4)],
        out_specs=[row, row],
        out_shape=[jax.ShapeDtypeStruct((t, d), F32), jax.ShapeDtypeStruct((t, d), BF16)],
        compiler_params=_params("arbitrary"),
        name="mid_epilogue",
    )(x2, tproj, g_post.reshape(1, d), mod3, g_pre.reshape(1, d), mod3, mod3)


FFN_BM = 1024
FFN_BH = 256
EPI_ROWS = 32
EPI_SLABS = FFN_BM // EPI_ROWS


def _ffn_kernel(ni, h_ref, wg_ref, wu_ref, wd_ref, x1_ref, gpost_ref, gt_ref, o_ref,
                acc0_ref, acc1_ref):
    i = pl.program_id(0)
    j = pl.program_id(1)
    accs = (acc0_ref, acc1_ref)
    odd = (i % 2) == 1

    def accumulate(acc_ref):
        h = h_ref[...]
        a = jnp.dot(h, wg_ref[...], preferred_element_type=F32)
        b = jnp.dot(h, wu_ref[...], preferred_element_type=F32)
        g = (jax.nn.silu(a) * b).astype(BF16)
        acc_ref[...] += jnp.dot(g, wd_ref[...], preferred_element_type=F32)

    def write_rows(acc_ref):
        r = pl.multiple_of(jnp.minimum(j, EPI_SLABS - 1) * EPI_ROWS, EPI_ROWS)
        f = acc_ref[pl.ds(r, EPI_ROWS), :]
        o_ref[...] = x1_ref[...] + gt_ref[...] * _rms(f, gpost_ref[...])

    for par in (0, 1):
        is_par = odd if par else jnp.logical_not(odd)

        @pl.when((j == 0) & (i < ni) & is_par)
        def _():
            accs[par][...] = jnp.zeros(accs[par].shape, F32)

    @pl.when(i == 0)
    def _():
        accumulate(accs[0])

    for par in (0, 1):
        is_par = odd if par else jnp.logical_not(odd)

        @pl.when((i > 0) & (i < ni) & is_par)
        def _():
            write_rows(accs[1 - par])
            accumulate(accs[par])

    @pl.when(i == ni)
    def _():
        write_rows(accs[(ni - 1) % 2])


def _ffn(h2, wg, wu, wd, x1, g_post, mod3, seq):
    t, d = h2.shape
    hid = wg.shape[1]
    ni, nj = t // FFN_BM, hid // FFN_BH
    bpb = seq // FFN_BM

    def row_block(i):
        return jnp.minimum(i, ni - 1)

    def hid_block(i, j):
        return jnp.where(i == ni, nj - 1, j)

    def slab(i, j):
        return (jnp.where(i == 0, 0, (i - 1) * EPI_SLABS + jnp.minimum(j, EPI_SLABS - 1)), 0)

    def gate(i, j):
        return ((jnp.maximum(i - 1, 0) // bpb) * N_MOD + 5, 0, 0)

    return pl.pallas_call(
        functools.partial(_ffn_kernel, ni),
        grid=(ni + 1, nj),
        in_specs=[pl.BlockSpec((FFN_BM, d), lambda i, j: (row_block(i), 0),
                               pipeline_mode=pl.Buffered(1)),
                  pl.BlockSpec((d, FFN_BH), lambda i, j: (0, hid_block(i, j))),
                  pl.BlockSpec((d, FFN_BH), lambda i, j: (0, hid_block(i, j))),
                  pl.BlockSpec((FFN_BH, d), lambda i, j: (hid_block(i, j), 0)),
                  pl.BlockSpec((EPI_ROWS, d), slab),
                  pl.BlockSpec((1, d), lambda i, j: (0, 0)),
                  pl.BlockSpec((None, 1, d), gate)],
        out_specs=pl.BlockSpec((EPI_ROWS, d), slab),
        out_shape=jax.ShapeDtypeStruct((t, d), F32),
        scratch_shapes=[pltpu.VMEM((FFN_BM, d), F32), pltpu.VMEM((FFN_BM, d), F32)],
        compiler_params=_params("arbitrary", "arbitrary"),
        name="ffn",
    )(h2, wg, wu, wd, x1, g_post.reshape(1, d), mod3)


def _rope_tables(positions):
    half = ROT_DIM // 2
    inv_freq = ROPE_THETA ** (-jnp.arange(0, ROT_DIM, 2, dtype=F32) / ROT_DIM)
    ang = positions.reshape(-1).astype(F32)[:, None] * inv_freq
    cos, sin = jnp.cos(ang), jnp.sin(ang)
    t = ang.shape[0]
    ones = jnp.ones((t, HEAD_DIM - ROT_DIM), F32)
    zeros_h = jnp.zeros((t, half), F32)
    zeros_r = jnp.zeros((t, HEAD_DIM - ROT_DIM), F32)
    c = jnp.concatenate([cos, cos, ones], axis=1)
    s1 = jnp.concatenate([-sin, zeros_h, zeros_r], axis=1)
    s2 = jnp.concatenate([zeros_h, sin, zeros_r], axis=1)
    return c, s1, s2


def kernel(x, c, positions, w_ada, b_ada, g_pre_mix, w_in, attn_sinks, sgu_ln_g, sgu_ln_b, sgu_w, sgu_b, w_proj_sgu, w_proj_attn, w_out, g_post_mix, g_pre_ffn, w_gate, w_up, w_down, g_post_ffn):
    bsz, seq, d = x.shape
    depth = w_ada.shape[0]
    t = bsz * seq
    xcur = x.reshape(t, d)
    rope_c, rope_s1, rope_s2 = _rope_tables(positions)
    c_pad = jnp.pad(c, ((0, 8 - bsz), (0, 0)))
    for l in range(depth):
        mod = _ada_mod(c_pad, w_ada[l], b_ada[l])[:bsz]
        mod3 = mod.reshape(bsz * N_MOD, 1, d)

        h1 = _prenorm(xcur, g_pre_mix[l], mod3, seq, 0, 1)
        (qkv,) = _in_proj(_rope_epilogue, h1, w_in[l], 0, QKV_WIDTH,
                          tables=(rope_c, rope_s1, rope_s2),
                          extra_scratch=(pltpu.VMEM((PROJ_BM, PROJ_BN), F32),),
                          name="in_proj_qkv")
        uv, wd_bf = _in_proj(functools.partial(_act_epilogue, _gelu), h1, w_in[l], QKV_WIDTH,
                             2 * SGU_WIDTH, casts=((w_down[l], CAST_ROWS_DOWN),),
                             name="in_proj_uv")
        gates, wg_bf, wu_bf = _in_proj(functools.partial(_act_epilogue, lambda z: z), h1,
                                       w_in[l], QKV_WIDTH + 2 * SGU_WIDTH, 2 * d,
                                       casts=((w_gate[l], CAST_ROWS_UP), (w_up[l], CAST_ROWS_UP)),
                                       name="in_proj_gates")

        attn_act = _attention(qkv, attn_sinks[l], seq)
        sgu_act = _sgu(uv, sgu_ln_g[l], sgu_ln_b[l], sgu_w[l], sgu_b[l])
        merged = _merge(sgu_act, attn_act, gates, w_proj_sgu[l], w_proj_attn[l])
        tproj = _out_proj(merged, w_out[l])
        x1, h2 = _mid_epilogue(xcur, tproj, g_post_mix[l], g_pre_ffn[l], mod3, seq)

        xcur = _ffn(h2, wg_bf, wu_bf, wd_bf, x1, g_post_ffn[l], mod3, seq)
    return xcur.reshape(bsz, seq, d)
```

```python
import functools

import jax
import jax.numpy as jnp
from jax import lax
from jax.experimental import pallas as pl
from jax.experimental.pallas import tpu as pltpu

D_MODEL = 4096
HEAD_DIM = 128
N_Q_HEADS = 16
N_KV_HEADS = 4
Q_PER_KV = 4
ATTN_WIDTH = N_Q_HEADS * HEAD_DIM
KV_WIDTH = N_KV_HEADS * HEAD_DIM
QKV_WIDTH = ATTN_WIDTH + 2 * KV_WIDTH
WINDOW = 128
BLOCK = 128
ROT_DIM = 32
ROPE_THETA = 500000.0
SGU_CHUNK = 128
SGU_GROUPS = 16
SGU_WIDTH = 2048
FFN_HIDDEN = 11008
N_MOD = 6
RMS_EPS = 1e-6
LN_EPS = 1e-5

VMEM_LIMIT_BYTES = 56 * 1024 * 1024

BF16 = jnp.bfloat16
F32 = jnp.float32


def _params(*semantics):
    return pltpu.CompilerParams(dimension_semantics=semantics,
                                vmem_limit_bytes=VMEM_LIMIT_BYTES)


def _rms(v, g):
    return v * lax.rsqrt(jnp.mean(v * v, axis=-1, keepdims=True) + RMS_EPS) * g


ADA_BN = 1024


def _ada_kernel(c_ref, w_ref, b_ref, o_ref):
    c_act = jax.nn.silu(c_ref[...]).astype(BF16)
    o_ref[...] = jnp.dot(c_act, w_ref[...].astype(BF16),
                         preferred_element_type=F32) + b_ref[...]


def _ada_mod(c_pad, w_ada, b_ada):
    rows, d = c_pad.shape
    n = w_ada.shape[1]
    return pl.pallas_call(
        _ada_kernel,
        grid=(n // ADA_BN,),
        in_specs=[pl.BlockSpec((rows, d), lambda j: (0, 0)),
                  pl.BlockSpec((d, ADA_BN), lambda j: (0, j)),
                  pl.BlockSpec((1, ADA_BN), lambda j: (0, j))],
        out_specs=pl.BlockSpec((rows, ADA_BN), lambda j: (0, j)),
        out_shape=jax.ShapeDtypeStruct((rows, n), F32),
        compiler_params=_params("arbitrary"),
        name="ada_mod",
    )(c_pad, w_ada, b_ada.reshape(1, n))


NORM_BM = 512


def _prenorm_kernel(x_ref, g_ref, sh_ref, sc_ref, o_ref):
    y = _rms(x_ref[...], g_ref[...])
    o_ref[...] = (y * (1.0 + sc_ref[...]) + sh_ref[...]).astype(o_ref.dtype)


def _mod_spec(blocks_per_batch, m):
    return pl.BlockSpec((None, 1, D_MODEL),
                        lambda i: ((i // blocks_per_batch) * N_MOD + m, 0, 0))


def _prenorm(x2, g, mod3, seq, m_shift, m_scale):
    t, d = x2.shape
    bpb = seq // NORM_BM
    return pl.pallas_call(
        _prenorm_kernel,
        grid=(t // NORM_BM,),
        in_specs=[pl.BlockSpec((NORM_BM, d), lambda i: (i, 0)),
                  pl.BlockSpec((1, d), lambda i: (0, 0)),
                  _mod_spec(bpb, m_shift), _mod_spec(bpb, m_scale)],
        out_specs=pl.BlockSpec((NORM_BM, d), lambda i: (i, 0)),
        out_shape=jax.ShapeDtypeStruct((t, d), BF16),
        compiler_params=_params("arbitrary"),
        name="prenorm",
    )(x2, g.reshape(1, d), mod3, mod3)


PROJ_BM = 1024
PROJ_BN = 512


def _cast_weight_once(w_ref, wbf_ref):
    @pl.when(pl.program_id(1) == 0)
    def _():
        wbf_ref[...] = w_ref[...].astype(BF16)


ROPE_ROWS = 128
CAST_ROWS_UP = 16
CAST_ROWS_DOWN = 128
CAST_ROWS_PROJ = 32
CAST_ROWS_OUT = 64


def _rope_epilogue(z, o_ref, tables, scratch):
    c_ref, s1_ref, s2_ref = tables
    (z_ref,) = scratch
    z_ref[...] = z
    is_v = pl.program_id(0) >= (ATTN_WIDTH + KV_WIDTH) // PROJ_BN
    half = ROT_DIM // 2

    @pl.when(jnp.logical_not(is_v))
    def _():
        for r in range(0, PROJ_BM, ROPE_ROWS):
            c = c_ref[r:r + ROPE_ROWS, :]
            s1 = s1_ref[r:r + ROPE_ROWS, :]
            s2 = s2_ref[r:r + ROPE_ROWS, :]
            for hd in range(0, PROJ_BN, HEAD_DIM):
                zh = z_ref[r:r + ROPE_ROWS, hd:hd + HEAD_DIM]
                up = pltpu.roll(zh, HEAD_DIM - half, 1)
                down = pltpu.roll(zh, half, 1)
                o_ref[r:r + ROPE_ROWS, hd:hd + HEAD_DIM] = (
                    zh * c + up * s1 + down * s2).astype(o_ref.dtype)

    @pl.when(is_v)
    def _():
        o_ref[...] = z_ref[...].astype(o_ref.dtype)


def _act_epilogue(act, z, o_ref, tables, scratch):
    o_ref[...] = act(z).astype(o_ref.dtype)


def _gelu(z):
    return 0.5 * z * (1.0 + lax.erf(z * (0.5 ** 0.5)))


def _proj_kernel(epilogue, n_tables, n_casts, *refs):
    h_ref, w_ref = refs[:2]
    tables = refs[2:2 + n_tables]
    cast_in = refs[2 + n_tables:2 + n_tables + n_casts]
    o_ref = refs[2 + n_tables + n_casts]
    cast_out = refs[3 + n_tables + n_casts:3 + n_tables + 2 * n_casts]
    wbf_ref = refs[3 + n_tables + 2 * n_casts]
    scratch = refs[4 + n_tables + 2 * n_casts:]
    _cast_weight_once(w_ref, wbf_ref)
    for src, dst in zip(cast_in, cast_out):
        dst[...] = src[...].astype(BF16)
    z = jnp.dot(h_ref[...], wbf_ref[...], preferred_element_type=F32)
    epilogue(z, o_ref, tables, scratch)


def _in_proj(epilogue, h, w_in, col_off, width, tables=(), casts=(), extra_scratch=(), name=None):
    t, d = h.shape
    nj, ni = width // PROJ_BN, t // PROJ_BM
    off_blocks = col_off // PROJ_BN
    table_specs = [pl.BlockSpec((PROJ_BM, a.shape[1]), lambda j, i: (i, 0)) for a in tables]
    cast_specs = []
    for a, rows in casts:
        n_slices = a.shape[0] // rows
        assert n_slices * rows == a.shape[0] and n_slices <= nj * ni, (a.shape, rows)
        cast_specs.append(pl.BlockSpec(
            (rows, a.shape[1]),
            functools.partial(lambda j, i, last: (jnp.minimum(j * ni + i, last), 0), last=n_slices - 1)))
    casts = [a for a, _ in casts]
    return pl.pallas_call(
        functools.partial(_proj_kernel, epilogue, len(tables), len(casts)),
        grid=(nj, ni),
        in_specs=[pl.BlockSpec((PROJ_BM, d), lambda j, i: (i, 0)),
                  pl.BlockSpec((d, PROJ_BN), lambda j, i: (0, j + off_blocks))] + table_specs + cast_specs,
        out_specs=[pl.BlockSpec((PROJ_BM, PROJ_BN), lambda j, i: (i, j))] + cast_specs,
        out_shape=[jax.ShapeDtypeStruct((t, width), BF16)]
        + [jax.ShapeDtypeStruct(a.shape, BF16) for a in casts],
        scratch_shapes=[pltpu.VMEM((d, PROJ_BN), BF16), *extra_scratch],
        compiler_params=_params("arbitrary", "arbitrary"),
        name=name,
    )(h, w_in, *tables, *casts)


ATT_ROWS = 1024


def _attn_kernel(sink_ref, q_ref, kc_ref, kp_ref, vc_ref, vp_ref, o_ref, *, blocks_per_seq):
    r = pl.program_id(0)
    h = pl.program_id(1)
    seq_start = (r % blocks_per_seq) == 0
    qi = lax.broadcasted_iota(jnp.int32, (BLOCK, 2 * BLOCK), 0)
    ki = lax.broadcasted_iota(jnp.int32, (BLOCK, 2 * BLOCK), 1)
    diff = qi + BLOCK - ki
    band = (diff >= 0) & (diff < WINDOW)
    band_first = band & ((ki >= BLOCK) | jnp.logical_not(seq_start))
    sink_col = ki == 0
    v_row = lax.broadcasted_iota(jnp.int32, (2 * BLOCK, HEAD_DIM), 0)
    ones = jnp.ones((2 * BLOCK, HEAD_DIM), BF16)
    scale = HEAD_DIM ** -0.5
    for n in range(ATT_ROWS // BLOCK):
        lo = n * BLOCK
        q4 = jnp.concatenate(
            [q_ref[lo:lo + BLOCK, g * HEAD_DIM:(g + 1) * HEAD_DIM] for g in range(Q_PER_KV)], axis=0)
        if n == 0:
            kk = jnp.concatenate([kp_ref[...], kc_ref[0:BLOCK, :]], axis=0)
            vv = jnp.concatenate([vp_ref[...], vc_ref[0:BLOCK, :]], axis=0)
            valid = band_first
        else:
            kk = kc_ref[lo - BLOCK:lo + BLOCK, :]
            vv = vc_ref[lo - BLOCK:lo + BLOCK, :]
            valid = band
        v_ones = jnp.concatenate([jnp.where(v_row == 0, jnp.zeros_like(vv), vv), ones], axis=1)
        s = lax.dot_general(q4, kk, (((1,), (1,)), ((), ())), preferred_element_type=F32)
        p = []
        for g in range(Q_PER_KV):
            sg = s[g * BLOCK:(g + 1) * BLOCK, :] * scale
            sg = jnp.where(sink_col, sink_ref[h * Q_PER_KV + g], jnp.where(valid, sg, -jnp.inf))
            m = jnp.max(sg, axis=-1, keepdims=True)
            p.append(jnp.exp(sg - m).astype(BF16))
        o = jnp.dot(jnp.concatenate(p, axis=0), v_ones, preferred_element_type=F32)
        o = o[:, :HEAD_DIM] / o[:, HEAD_DIM:]
        for g in range(Q_PER_KV):
            o_ref[lo:lo + BLOCK, g * HEAD_DIM:(g + 1) * HEAD_DIM] = (
                o[g * BLOCK:(g + 1) * BLOCK, :].astype(o_ref.dtype))


def _attention(qkv, sinks, seq):
    t = qkv.shape[0]
    bps = seq // ATT_ROWS
    qw = Q_PER_KV * HEAD_DIM
    k_col = ATTN_WIDTH // HEAD_DIM
    v_col = (ATTN_WIDTH + KV_WIDTH) // HEAD_DIM
    sub = ATT_ROWS // BLOCK

    def prev(r):
        return jnp.maximum(r * sub - 1, 0)

    return pl.pallas_call(
        functools.partial(_attn_kernel, blocks_per_seq=bps),
        grid=(t // ATT_ROWS, N_KV_HEADS),
        in_specs=[pl.BlockSpec(memory_space=pltpu.SMEM),
                  pl.BlockSpec((ATT_ROWS, qw), lambda r, h: (r, h)),
                  pl.BlockSpec((ATT_ROWS, HEAD_DIM), lambda r, h: (r, k_col + h)),
                  pl.BlockSpec((BLOCK, HEAD_DIM), lambda r, h: (prev(r), k_col + h)),
                  pl.BlockSpec((ATT_ROWS, HEAD_DIM), lambda r, h: (r, v_col + h)),
                  pl.BlockSpec((BLOCK, HEAD_DIM), lambda r, h: (prev(r), v_col + h))],
        out_specs=pl.BlockSpec((ATT_ROWS, qw), lambda r, h: (r, h)),
        out_shape=jax.ShapeDtypeStruct((t, ATTN_WIDTH), BF16),
        compiler_params=_params("arbitrary", "arbitrary"),
        name="swa_attention",
    )(sinks, qkv, qkv, qkv, qkv, qkv)


SGU_ROWS = 512


def _sgu_kernel(u_ref, v_ref, lng_ref, lnb_ref, w_ref, b_ref, o_ref):
    ti = lax.broadcasted_iota(jnp.int32, (SGU_CHUNK, SGU_CHUNK), 0)
    si = lax.broadcasted_iota(jnp.int32, (SGU_CHUNK, SGU_CHUNK), 1)
    causal = si <= ti
    gw = SGU_WIDTH // SGU_GROUPS
    for c in range(SGU_ROWS // SGU_CHUNK):
        lo = c * SGU_CHUNK
        v = v_ref[lo:lo + SGU_CHUNK, :].astype(F32)
        mu = jnp.mean(v, axis=-1, keepdims=True)
        vc = v - mu
        vn = vc * lax.rsqrt(jnp.mean(vc * vc, axis=-1, keepdims=True) + LN_EPS)
        vn = (vn * lng_ref[...] + lnb_ref[...]).astype(BF16)
        for g in range(SGU_GROUPS):
            w = jnp.where(causal, w_ref[g], 0.0).astype(BF16)
            mixed = jnp.dot(w, vn[:, g * gw:(g + 1) * gw], preferred_element_type=F32) + b_ref[g]
            u = u_ref[lo:lo + SGU_CHUNK, g * gw:(g + 1) * gw].astype(F32)
            o_ref[lo:lo + SGU_CHUNK, g * gw:(g + 1) * gw] = (u * mixed).astype(o_ref.dtype)


def _sgu(uv, ln_g, ln_b, w_s, b_s):
    t = uv.shape[0]
    b_bcast = jnp.broadcast_to(b_s[:, :, None], (SGU_GROUPS, SGU_CHUNK, SGU_WIDTH // SGU_GROUPS))
    full3 = lambda i: (0, 0, 0)
    return pl.pallas_call(
        _sgu_kernel,
        grid=(t // SGU_ROWS,),
        in_specs=[pl.BlockSpec((SGU_ROWS, SGU_WIDTH), lambda i: (i, 0)),
                  pl.BlockSpec((SGU_ROWS, SGU_WIDTH), lambda i: (i, 1)),
                  pl.BlockSpec((1, SGU_WIDTH), lambda i: (0, 0)),
                  pl.BlockSpec((1, SGU_WIDTH), lambda i: (0, 0)),
                  pl.BlockSpec(w_s.shape, full3),
                  pl.BlockSpec(b_bcast.shape, full3)],
        out_specs=pl.BlockSpec((SGU_ROWS, SGU_WIDTH), lambda i: (i, 0)),
        out_shape=jax.ShapeDtypeStruct((t, SGU_WIDTH), BF16),
        compiler_params=_params("arbitrary"),
        name="sgu",
    )(uv, uv, ln_g.reshape(1, -1), ln_b.reshape(1, -1), w_s, b_bcast)


MIX_BM = 512
MIX_BK = 512
MIX_ROWS = MIX_BM // (D_MODEL // MIX_BK)


def _mix_kernel(ni, a_ref, b_ref, ga_ref, gb_ref, wa_ref, wb_ref, wo_ref, x_ref, gpost_ref,
                gt_ref, gpre_ref, sh_ref, sc_ref, x1_ref, h2_ref, acc0_ref, acc1_ref):
    i = pl.program_id(0)
    k = pl.program_id(1)
    accs = (acc0_ref, acc1_ref)
    odd = (i % 2) == 1

    def accumulate(acc_ref):
        ya = jnp.dot(a_ref[...], wa_ref[...], preferred_element_type=F32)
        yb = jnp.dot(b_ref[...], wb_ref[...], preferred_element_type=F32)
        ga = jax.nn.sigmoid(ga_ref[...].astype(F32))
        gb = jax.nn.sigmoid(gb_ref[...].astype(F32))
        merged = (ga * ya + gb * yb).astype(BF16)
        acc_ref[...] += jnp.dot(merged, wo_ref[...], preferred_element_type=F32)

    def write_rows(acc_ref):
        r = pl.multiple_of(k * MIX_ROWS, MIX_ROWS)
        tproj = acc_ref[pl.ds(r, MIX_ROWS), :]
        x1 = x_ref[...] + gt_ref[...] * _rms(tproj, gpost_ref[...])
        x1_ref[...] = x1
        h2_ref[...] = (_rms(x1, gpre_ref[...]) * (1.0 + sc_ref[...]) + sh_ref[...]).astype(h2_ref.dtype)

    for par in (0, 1):
        is_par = odd if par else jnp.logical_not(odd)

        @pl.when((k == 0) & (i < ni) & is_par)
        def _():
            accs[par][...] = jnp.zeros(accs[par].shape, F32)

    @pl.when(i == 0)
    def _():
        accumulate(accs[0])

    for par in (0, 1):
        is_par = odd if par else jnp.logical_not(odd)

        @pl.when((i > 0) & (i < ni) & is_par)
        def _():
            write_rows(accs[1 - par])
            accumulate(accs[par])

    @pl.when(i == ni)
    def _():
        write_rows(accs[(ni - 1) % 2])


def _mix(sgu_act, attn_act, gates, wps, wpa, wout, x2, g_post, g_pre, mod3, seq):
    t, ka = sgu_act.shape
    d = wout.shape[1]
    ni, nk = t // MIX_BM, d // MIX_BK
    bpb = seq // MIX_BM

    def row_block(i):
        return jnp.minimum(i, ni - 1)

    def col_block(i, k):
        return jnp.where(i == ni, nk - 1, k)

    def slab(i, k):
        return (jnp.where(i == 0, 0, (i - 1) * nk + k), 0)

    def mod(m):
        return pl.BlockSpec((None, 1, d),
                            lambda i, k: ((jnp.maximum(i - 1, 0) // bpb) * N_MOD + m, 0, 0))

    vec = pl.BlockSpec((1, d), lambda i, k: (0, 0))
    return pl.pallas_call(
        functools.partial(_mix_kernel, ni),
        grid=(ni + 1, nk),
        in_specs=[pl.BlockSpec((MIX_BM, ka), lambda i, k: (row_block(i), 0)),
                  pl.BlockSpec((MIX_BM, ka), lambda i, k: (row_block(i), 0)),
                  pl.BlockSpec((MIX_BM, MIX_BK), lambda i, k: (row_block(i), col_block(i, k))),
                  pl.BlockSpec((MIX_BM, MIX_BK), lambda i, k: (row_block(i), col_block(i, k) + nk)),
                  pl.BlockSpec((ka, MIX_BK), lambda i, k: (0, col_block(i, k))),
                  pl.BlockSpec((ka, MIX_BK), lambda i, k: (0, col_block(i, k))),
                  pl.BlockSpec((MIX_BK, d), lambda i, k: (col_block(i, k), 0)),
                  pl.BlockSpec((MIX_ROWS, d), slab),
                  vec, mod(2), vec, mod(3), mod(4)],
        out_specs=[pl.BlockSpec((MIX_ROWS, d), slab), pl.BlockSpec((MIX_ROWS, d), slab)],
        out_shape=[jax.ShapeDtypeStruct((t, d), F32), jax.ShapeDtypeStruct((t, d), BF16)],
        scratch_shapes=[pltpu.VMEM((MIX_BM, d), F32), pltpu.VMEM((MIX_BM, d), F32)],
        compiler_params=_params("arbitrary", "arbitrary"),
        name="mix_out",
    )(sgu_act, attn_act, gates, gates, wps, wpa, wout, x2, g_post.reshape(1, d), mod3,
      g_pre.reshape(1, d), mod3, mod3)


FFN_BM = 1024
FFN_BH = 256
EPI_ROWS = 32
EPI_SLABS = FFN_BM // EPI_ROWS


def _ffn_kernel(ni, h_ref, wg_ref, wu_ref, wd_ref, x1_ref, gpost_ref, gt_ref, o_ref,
                acc0_ref, acc1_ref):
    i = pl.program_id(0)
    j = pl.program_id(1)
    accs = (acc0_ref, acc1_ref)
    odd = (i % 2) == 1

    def accumulate(acc_ref):
        h = h_ref[...]
        a = jnp.dot(h, wg_ref[...], preferred_element_type=F32)
        b = jnp.dot(h, wu_ref[...], preferred_element_type=F32)
        g = (jax.nn.silu(a) * b).astype(BF16)
        acc_ref[...] += jnp.dot(g, wd_ref[...], preferred_element_type=F32)

    def write_rows(acc_ref):
        r = pl.multiple_of(jnp.minimum(j, EPI_SLABS - 1) * EPI_ROWS, EPI_ROWS)
        f = acc_ref[pl.ds(r, EPI_ROWS), :]
        o_ref[...] = x1_ref[...] + gt_ref[...] * _rms(f, gpost_ref[...])

    for par in (0, 1):
        is_par = odd if par else jnp.logical_not(odd)

        @pl.when((j == 0) & (i < ni) & is_par)
        def _():
            accs[par][...] = jnp.zeros(accs[par].shape, F32)

    @pl.when(i == 0)
    def _():
        accumulate(accs[0])

    for par in (0, 1):
        is_par = odd if par else jnp.logical_not(odd)

        @pl.when((i > 0) & (i < ni) & is_par)
        def _():
            write_rows(accs[1 - par])
            accumulate(accs[par])

    @pl.when(i == ni)
    def _():
        write_rows(accs[(ni - 1) % 2])


def _ffn(h2, wg, wu, wd, x1, g_post, mod3, seq):
    t, d = h2.shape
    hid = wg.shape[1]
    ni, nj = t // FFN_BM, hid // FFN_BH
    bpb = seq // FFN_BM

    def row_block(i):
        return jnp.minimum(i, ni - 1)

    def hid_block(i, j):
        return jnp.where(i == ni, nj - 1, j)

    def slab(i, j):
        return (jnp.where(i == 0, 0, (i - 1) * EPI_SLABS + jnp.minimum(j, EPI_SLABS - 1)), 0)

    def gate(i, j):
        return ((jnp.maximum(i - 1, 0) // bpb) * N_MOD + 5, 0, 0)

    return pl.pallas_call(
        functools.partial(_ffn_kernel, ni),
        grid=(ni + 1, nj),
        in_specs=[pl.BlockSpec((FFN_BM, d), lambda i, j: (row_block(i), 0),
                               pipeline_mode=pl.Buffered(1)),
                  pl.BlockSpec((d, FFN_BH), lambda i, j: (0, hid_block(i, j))),
                  pl.BlockSpec((d, FFN_BH), lambda i, j: (0, hid_block(i, j))),
                  pl.BlockSpec((FFN_BH, d), lambda i, j: (hid_block(i, j), 0)),
                  pl.BlockSpec((EPI_ROWS, d), slab),
                  pl.BlockSpec((1, d), lambda i, j: (0, 0)),
                  pl.BlockSpec((None, 1, d), gate)],
        out_specs=pl.BlockSpec((EPI_ROWS, d), slab),
        out_shape=jax.ShapeDtypeStruct((t, d), F32),
        scratch_shapes=[pltpu.VMEM((FFN_BM, d), F32), pltpu.VMEM((FFN_BM, d), F32)],
        compiler_params=_params("arbitrary", "arbitrary"),
        name="ffn",
    )(h2, wg, wu, wd, x1, g_post.reshape(1, d), mod3)


def _rope_tables(positions):
    half = ROT_DIM // 2
    inv_freq = ROPE_THETA ** (-jnp.arange(0, ROT_DIM, 2, dtype=F32) / ROT_DIM)
    ang = positions.reshape(-1).astype(F32)[:, None] * inv_freq
    cos, sin = jnp.cos(ang), jnp.sin(ang)
    t = ang.shape[0]
    ones = jnp.ones((t, HEAD_DIM - ROT_DIM), F32)
    zeros_h = jnp.zeros((t, half), F32)
    zeros_r = jnp.zeros((t, HEAD_DIM - ROT_DIM), F32)
    c = jnp.concatenate([cos, cos, ones], axis=1)
    s1 = jnp.concatenate([-sin, zeros_h, zeros_r], axis=1)
    s2 = jnp.concatenate([zeros_h, sin, zeros_r], axis=1)
    return c, s1, s2


def kernel(x, c, positions, w_ada, b_ada, g_pre_mix, w_in, attn_sinks, sgu_ln_g, sgu_ln_b, sgu_w, sgu_b, w_proj_sgu, w_proj_attn, w_out, g_post_mix, g_pre_ffn, w_gate, w_up, w_down, g_post_ffn):
    bsz, seq, d = x.shape
    depth = w_ada.shape[0]
    t = bsz * seq
    xcur = x.reshape(t, d)
    rope_c, rope_s1, rope_s2 = _rope_tables(positions)
    c_pad = jnp.pad(c, ((0, 8 - bsz), (0, 0)))
    for l in range(depth):
        mod = _ada_mod(c_pad, w_ada[l], b_ada[l])[:bsz]
        mod3 = mod.reshape(bsz * N_MOD, 1, d)

        h1 = _prenorm(xcur, g_pre_mix[l], mod3, seq, 0, 1)
        qkv, wps_bf, wpa_bf, wout_bf = _in_proj(
            _rope_epilogue, h1, w_in[l], 0, QKV_WIDTH, tables=(rope_c, rope_s1, rope_s2),
            casts=((w_proj_sgu[l], CAST_ROWS_PROJ), (w_proj_attn[l], CAST_ROWS_PROJ),
                   (w_out[l], CAST_ROWS_OUT)),
            extra_scratch=(pltpu.VMEM((PROJ_BM, PROJ_BN), F32),), name="in_proj_qkv")
        uv, wd_bf = _in_proj(functools.partial(_act_epilogue, _gelu), h1, w_in[l], QKV_WIDTH,
                             2 * SGU_WIDTH, casts=((w_down[l], CAST_ROWS_DOWN),),
                             name="in_proj_uv")
        gates, wg_bf, wu_bf = _in_proj(functools.partial(_act_epilogue, lambda z: z), h1,
                                       w_in[l], QKV_WIDTH + 2 * SGU_WIDTH, 2 * d,
                                       casts=((w_gate[l], CAST_ROWS_UP), (w_up[l], CAST_ROWS_UP)),
                                       name="in_proj_gates")

        attn_act = _attention(qkv, attn_sinks[l], seq)
        sgu_act = _sgu(uv, sgu_ln_g[l], sgu_ln_b[l], sgu_w[l], sgu_b[l])
        x1, h2 = _mix(sgu_act, attn_act, gates, wps_bf, wpa_bf, wout_bf, xcur,
                      g_post_mix[l], g_pre_ffn[l], mod3, seq)

        xcur = _ffn(h2, wg_bf, wu_bf, wd_bf, x1, g_post_ffn[l], mod3, seq)
    return xcur.reshape(bsz, seq, d)
```

```python
import functools

import jax
import jax.numpy as jnp
from jax import lax
from jax.experimental import pallas as pl
from jax.experimental.pallas import tpu as pltpu

D_MODEL = 4096
HEAD_DIM = 128
N_Q_HEADS = 16
N_KV_HEADS = 4
Q_PER_KV = 4
ATTN_WIDTH = N_Q_HEADS * HEAD_DIM
KV_WIDTH = N_KV_HEADS * HEAD_DIM
QKV_WIDTH = ATTN_WIDTH + 2 * KV_WIDTH
WINDOW = 128
BLOCK = 128
ROT_DIM = 32
ROPE_THETA = 500000.0
SGU_CHUNK = 128
SGU_GROUPS = 16
SGU_WIDTH = 2048
FFN_HIDDEN = 11008
N_MOD = 6
RMS_EPS = 1e-6
LN_EPS = 1e-5

VMEM_LIMIT_BYTES = 56 * 1024 * 1024

BF16 = jnp.bfloat16
F32 = jnp.float32


def _params(*semantics):
    return pltpu.CompilerParams(dimension_semantics=semantics,
                                vmem_limit_bytes=VMEM_LIMIT_BYTES)


def _rms(v, g):
    return v * lax.rsqrt(jnp.mean(v * v, axis=-1, keepdims=True) + RMS_EPS) * g


ADA_BN = 1024


def _ada_kernel(c_ref, w_ref, b_ref, o_ref):
    c_act = jax.nn.silu(c_ref[...]).astype(BF16)
    o_ref[...] = jnp.dot(c_act, w_ref[...].astype(BF16),
                         preferred_element_type=F32) + b_ref[...]


def _ada_mod(c_pad, w_ada, b_ada):
    rows, d = c_pad.shape
    n = w_ada.shape[1]
    return pl.pallas_call(
        _ada_kernel,
        grid=(n // ADA_BN,),
        in_specs=[pl.BlockSpec((rows, d), lambda j: (0, 0)),
                  pl.BlockSpec((d, ADA_BN), lambda j: (0, j)),
                  pl.BlockSpec((1, ADA_BN), lambda j: (0, j))],
        out_specs=pl.BlockSpec((rows, ADA_BN), lambda j: (0, j)),
        out_shape=jax.ShapeDtypeStruct((rows, n), F32),
        compiler_params=_params("arbitrary"),
        name="ada_mod",
    )(c_pad, w_ada, b_ada.reshape(1, n))


NORM_BM = 512


def _prenorm_kernel(x_ref, g_ref, sh_ref, sc_ref, o_ref):
    y = _rms(x_ref[...], g_ref[...])
    o_ref[...] = (y * (1.0 + sc_ref[...]) + sh_ref[...]).astype(o_ref.dtype)


def _mod_spec(blocks_per_batch, m):
    return pl.BlockSpec((None, 1, D_MODEL),
                        lambda i: ((i // blocks_per_batch) * N_MOD + m, 0, 0))


def _prenorm(x2, g, mod3, seq, m_shift, m_scale):
    t, d = x2.shape
    bpb = seq // NORM_BM
    return pl.pallas_call(
        _prenorm_kernel,
        grid=(t // NORM_BM,),
        in_specs=[pl.BlockSpec((NORM_BM, d), lambda i: (i, 0)),
                  pl.BlockSpec((1, d), lambda i: (0, 0)),
                  _mod_spec(bpb, m_shift), _mod_spec(bpb, m_scale)],
        out_specs=pl.BlockSpec((NORM_BM, d), lambda i: (i, 0)),
        out_shape=jax.ShapeDtypeStruct((t, d), BF16),
        compiler_params=_params("arbitrary"),
        name="prenorm",
    )(x2, g.reshape(1, d), mod3, mod3)


PROJ_BM = 1024
PROJ_BN = 512


def _cast_weight_once(w_ref, wbf_ref):
    @pl.when(pl.program_id(1) == 0)
    def _():
        wbf_ref[...] = w_ref[...].astype(BF16)


ROPE_ROWS = 128
CAST_ROWS_UP = 16
CAST_ROWS_DOWN = 128
CAST_ROWS_PROJ = 32
CAST_ROWS_OUT = 64


def _rope_epilogue(z, o_ref, tables, scratch):
    c_ref, s1_ref, s2_ref = tables
    (z_ref,) = scratch
    z_ref[...] = z
    is_v = pl.program_id(0) >= (ATTN_WIDTH + KV_WIDTH) // PROJ_BN
    half = ROT_DIM // 2

    @pl.when(jnp.logical_not(is_v))
    def _():
        for r in range(0, PROJ_BM, ROPE_ROWS):
            c = c_ref[r:r + ROPE_ROWS, :]
            s1 = s1_ref[r:r + ROPE_ROWS, :]
            s2 = s2_ref[r:r + ROPE_ROWS, :]
            for hd in range(0, PROJ_BN, HEAD_DIM):
                zh = z_ref[r:r + ROPE_ROWS, hd:hd + HEAD_DIM]
                up = pltpu.roll(zh, HEAD_DIM - half, 1)
                down = pltpu.roll(zh, half, 1)
                o_ref[r:r + ROPE_ROWS, hd:hd + HEAD_DIM] = (
                    zh * c + up * s1 + down * s2).astype(o_ref.dtype)

    @pl.when(is_v)
    def _():
        o_ref[...] = z_ref[...].astype(o_ref.dtype)


def _act_epilogue(act, z, o_ref, tables, scratch):
    o_ref[...] = act(z).astype(o_ref.dtype)


def _gelu(z):
    return 0.5 * z * (1.0 + lax.erf(z * (0.5 ** 0.5)))


def _proj_kernel(epilogue, n_tables, n_casts, *refs):
    h_ref, w_ref = refs[:2]
    tables = refs[2:2 + n_tables]
    cast_in = refs[2 + n_tables:2 + n_tables + n_casts]
    o_ref = refs[2 + n_tables + n_casts]
    cast_out = refs[3 + n_tables + n_casts:3 + n_tables + 2 * n_casts]
    wbf_ref = refs[3 + n_tables + 2 * n_casts]
    scratch = refs[4 + n_tables + 2 * n_casts:]
    _cast_weight_once(w_ref, wbf_ref)
    for src, dst in zip(cast_in, cast_out):
        dst[...] = src[...].astype(BF16)
    z = jnp.dot(h_ref[...], wbf_ref[...], preferred_element_type=F32)
    epilogue(z, o_ref, tables, scratch)


def _in_proj(epilogue, h, w_in, col_off, width, tables=(), casts=(), extra_scratch=(), name=None):
    t, d = h.shape
    nj, ni = width // PROJ_BN, t // PROJ_BM
    off_blocks = col_off // PROJ_BN
    table_specs = [pl.BlockSpec((PROJ_BM, a.shape[1]), lambda j, i: (i, 0)) for a in tables]
    cast_specs = []
    for a, rows in casts:
        n_slices = a.shape[0] // rows
        assert n_slices * rows == a.shape[0] and n_slices <= nj * ni, (a.shape, rows)
        cast_specs.append(pl.BlockSpec(
            (rows, a.shape[1]),
            functools.partial(lambda j, i, last: (jnp.minimum(j * ni + i, last), 0), last=n_slices - 1)))
    casts = [a for a, _ in casts]
    return pl.pallas_call(
        functools.partial(_proj_kernel, epilogue, len(tables), len(casts)),
        grid=(nj, ni),
        in_specs=[pl.BlockSpec((PROJ_BM, d), lambda j, i: (i, 0)),
                  pl.BlockSpec((d, PROJ_BN), lambda j, i: (0, j + off_blocks))] + table_specs + cast_specs,
        out_specs=[pl.BlockSpec((PROJ_BM, PROJ_BN), lambda j, i: (i, j))] + cast_specs,
        out_shape=[jax.ShapeDtypeStruct((t, width), BF16)]
        + [jax.ShapeDtypeStruct(a.shape, BF16) for a in casts],
        scratch_shapes=[pltpu.VMEM((d, PROJ_BN), BF16), *extra_scratch],
        compiler_params=_params("arbitrary", "arbitrary"),
        name=name,
    )(h, w_in, *tables, *casts)


ATT_ROWS = 1024


def _attn_kernel(sink_ref, q_ref, kc_ref, kp_ref, vc_ref, vp_ref, o_ref, *, blocks_per_seq):
    r = pl.program_id(0)
    h = pl.program_id(1)
    seq_start = (r % blocks_per_seq) == 0
    qi = lax.broadcasted_iota(jnp.int32, (BLOCK, 2 * BLOCK), 0)
    ki = lax.broadcasted_iota(jnp.int32, (BLOCK, 2 * BLOCK), 1)
    diff = qi + BLOCK - ki
    band = (diff >= 0) & (diff < WINDOW)
    band_first = band & ((ki >= BLOCK) | jnp.logical_not(seq_start))
    sink_col = ki == 0
    v_row = lax.broadcasted_iota(jnp.int32, (2 * BLOCK, HEAD_DIM), 0)
    ones = jnp.ones((2 * BLOCK, HEAD_DIM), BF16)
    scale = HEAD_DIM ** -0.5
    for n in range(ATT_ROWS // BLOCK):
        lo = n * BLOCK
        q4 = jnp.concatenate(
            [q_ref[lo:lo + BLOCK, g * HEAD_DIM:(g + 1) * HEAD_DIM] for g in range(Q_PER_KV)], axis=0)
        if n == 0:
            kk = jnp.concatenate([kp_ref[...], kc_ref[0:BLOCK, :]], axis=0)
            vv = jnp.concatenate([vp_ref[...], vc_ref[0:BLOCK, :]], axis=0)
            valid = band_first
        else:
            kk = kc_ref[lo - BLOCK:lo + BLOCK, :]
            vv = vc_ref[lo - BLOCK:lo + BLOCK, :]
            valid = band
        v_ones = jnp.concatenate([jnp.where(v_row == 0, jnp.zeros_like(vv), vv), ones], axis=1)
        s = lax.dot_general(q4, kk, (((1,), (1,)), ((), ())), preferred_element_type=F32)
        p = []
        for g in range(Q_PER_KV):
            sg = s[g * BLOCK:(g + 1) * BLOCK, :] * scale
            sg = jnp.where(sink_col, sink_ref[h * Q_PER_KV + g], jnp.where(valid, sg, -jnp.inf))
            m = jnp.max(sg, axis=-1, keepdims=True)
            p.append(jnp.exp(sg - m).astype(BF16))
        o = jnp.dot(jnp.concatenate(p, axis=0), v_ones, preferred_element_type=F32)
        o = o[:, :HEAD_DIM] / o[:, HEAD_DIM:]
        for g in range(Q_PER_KV):
            o_ref[lo:lo + BLOCK, g * HEAD_DIM:(g + 1) * HEAD_DIM] = (
                o[g * BLOCK:(g + 1) * BLOCK, :].astype(o_ref.dtype))


def _attention(qkv, sinks, seq):
    t = qkv.shape[0]
    bps = seq // ATT_ROWS
    qw = Q_PER_KV * HEAD_DIM
    k_col = ATTN_WIDTH // HEAD_DIM
    v_col = (ATTN_WIDTH + KV_WIDTH) // HEAD_DIM
    sub = ATT_ROWS // BLOCK

    def prev(r):
        return jnp.maximum(r * sub - 1, 0)

    return pl.pallas_call(
        functools.partial(_attn_kernel, blocks_per_seq=bps),
        grid=(t // ATT_ROWS, N_KV_HEADS),
        in_specs=[pl.BlockSpec(memory_space=pltpu.SMEM),
                  pl.BlockSpec((ATT_ROWS, qw), lambda r, h: (r, h)),
                  pl.BlockSpec((ATT_ROWS, HEAD_DIM), lambda r, h: (r, k_col + h)),
                  pl.BlockSpec((BLOCK, HEAD_DIM), lambda r, h: (prev(r), k_col + h)),
                  pl.BlockSpec((ATT_ROWS, HEAD_DIM), lambda r, h: (r, v_col + h)),
                  pl.BlockSpec((BLOCK, HEAD_DIM), lambda r, h: (prev(r), v_col + h))],
        out_specs=pl.BlockSpec((ATT_ROWS, qw), lambda r, h: (r, h)),
        out_shape=jax.ShapeDtypeStruct((t, ATTN_WIDTH), BF16),
        compiler_params=_params("arbitrary", "arbitrary"),
        name="swa_attention",
    )(sinks, qkv, qkv, qkv, qkv, qkv)


SGU_ROWS = 512


def _sgu_kernel(u_ref, v_ref, lng_ref, lnb_ref, w_ref, b_ref, o_ref):
    ti = lax.broadcasted_iota(jnp.int32, (SGU_CHUNK, SGU_CHUNK), 0)
    si = lax.broadcasted_iota(jnp.int32, (SGU_CHUNK, SGU_CHUNK), 1)
    causal = si <= ti
    gw = SGU_WIDTH // SGU_GROUPS
    for c in range(SGU_ROWS // SGU_CHUNK):
        lo = c * SGU_CHUNK
        v = v_ref[lo:lo + SGU_CHUNK, :].astype(F32)
        mu = jnp.mean(v, axis=-1, keepdims=True)
        vc = v - mu
        vn = vc * lax.rsqrt(jnp.mean(vc * vc, axis=-1, keepdims=True) + LN_EPS)
        vn = (vn * lng_ref[...] + lnb_ref[...]).astype(BF16)
        for g in range(SGU_GROUPS):
            w = jnp.where(causal, w_ref[g], 0.0).astype(BF16)
            mixed = jnp.dot(w, vn[:, g * gw:(g + 1) * gw], preferred_element_type=F32) + b_ref[g]
            u = u_ref[lo:lo + SGU_CHUNK, g * gw:(g + 1) * gw].astype(F32)
            o_ref[lo:lo + SGU_CHUNK, g * gw:(g + 1) * gw] = (u * mixed).astype(o_ref.dtype)


def _sgu(uv, ln_g, ln_b, w_s, b_s):
    t = uv.shape[0]
    b_bcast = jnp.broadcast_to(b_s[:, :, None], (SGU_GROUPS, SGU_CHUNK, SGU_WIDTH // SGU_GROUPS))
    full3 = lambda i: (0, 0, 0)
    return pl.pallas_call(
        _sgu_kernel,
        grid=(t // SGU_ROWS,),
        in_specs=[pl.BlockSpec((SGU_ROWS, SGU_WIDTH), lambda i: (i, 0)),
                  pl.BlockSpec((SGU_ROWS, SGU_WIDTH), lambda i: (i, 1)),
                  pl.BlockSpec((1, SGU_WIDTH), lambda i: (0, 0)),
                  pl.BlockSpec((1, SGU_WIDTH), lambda i: (0, 0)),
                  pl.BlockSpec(w_s.shape, full3),
                  pl.BlockSpec(b_bcast.shape, full3)],
        out_specs=pl.BlockSpec((SGU_ROWS, SGU_WIDTH), lambda i: (i, 0)),
        out_shape=jax.ShapeDtypeStruct((t, SGU_WIDTH), BF16),
        compiler_params=_params("arbitrary"),
        name="sgu",
    )(uv, uv, ln_g.reshape(1, -1), ln_b.reshape(1, -1), w_s, b_bcast)


MIX_BM = 512
MIX_BK = 512
MIX_ROWS = MIX_BM // (D_MODEL // MIX_BK)


def _mix_kernel(ni, a_ref, b_ref, ga_ref, gb_ref, wa_ref, wb_ref, wo_ref, x_ref, gpost_ref,
                gt_ref, gpre_ref, sh_ref, sc_ref, x1_ref, h2_ref, acc0_ref, acc1_ref):
    i = pl.program_id(0)
    k = pl.program_id(1)
    accs = (acc0_ref, acc1_ref)
    odd = (i % 2) == 1

    def accumulate(acc_ref):
        ya = jnp.dot(a_ref[...], wa_ref[...], preferred_element_type=F32)
        yb = jnp.dot(b_ref[...], wb_ref[...], preferred_element_type=F32)
        ga = jax.nn.sigmoid(ga_ref[...].astype(F32))
        gb = jax.nn.sigmoid(gb_ref[...].astype(F32))
        merged = (ga * ya + gb * yb).astype(BF16)
        acc_ref[...] += jnp.dot(merged, wo_ref[...], preferred_element_type=F32)

    def write_rows(acc_ref):
        r = pl.multiple_of(k * MIX_ROWS, MIX_ROWS)
        tproj = acc_ref[pl.ds(r, MIX_ROWS), :]
        x1 = x_ref[...] + gt_ref[...] * _rms(tproj, gpost_ref[...])
        x1_ref[...] = x1
        h2_ref[...] = (_rms(x1, gpre_ref[...]) * (1.0 + sc_ref[...]) + sh_ref[...]).astype(h2_ref.dtype)

    for par in (0, 1):
        is_par = odd if par else jnp.logical_not(odd)

        @pl.when((k == 0) & (i < ni) & is_par)
        def _():
            accs[par][...] = jnp.zeros(accs[par].shape, F32)

    @pl.when(i == 0)
    def _():
        accumulate(accs[0])

    for par in (0, 1):
        is_par = odd if par else jnp.logical_not(odd)

        @pl.when((i > 0) & (i < ni) & is_par)
        def _():
            write_rows(accs[1 - par])
            accumulate(accs[par])

    @pl.when(i == ni)
    def _():
        write_rows(accs[(ni - 1) % 2])


def _mix(sgu_act, attn_act, gates, wps, wpa, wout, x2, g_post, g_pre, mod3, seq):
    t, ka = sgu_act.shape
    d = wout.shape[1]
    ni, nk = t // MIX_BM, d // MIX_BK
    bpb = seq // MIX_BM

    def row_block(i):
        return jnp.minimum(i, ni - 1)

    def col_block(i, k):
        return jnp.where(i == ni, nk - 1, k)

    def slab(i, k):
        return (jnp.where(i == 0, 0, (i - 1) * nk + k), 0)

    def mod(m):
        return pl.BlockSpec((None, 1, d),
                            lambda i, k: ((jnp.maximum(i - 1, 0) // bpb) * N_MOD + m, 0, 0))

    vec = pl.BlockSpec((1, d), lambda i, k: (0, 0))
    return pl.pallas_call(
        functools.partial(_mix_kernel, ni),
        grid=(ni + 1, nk),
        in_specs=[pl.BlockSpec((MIX_BM, ka), lambda i, k: (row_block(i), 0)),
                  pl.BlockSpec((MIX_BM, ka), lambda i, k: (row_block(i), 0)),
                  pl.BlockSpec((MIX_BM, MIX_BK), lambda i, k: (row_block(i), col_block(i, k))),
                  pl.BlockSpec((MIX_BM, MIX_BK), lambda i, k: (row_block(i), col_block(i, k) + nk)),
                  pl.BlockSpec((ka, MIX_BK), lambda i, k: (0, col_block(i, k))),
                  pl.BlockSpec((ka, MIX_BK), lambda i, k: (0, col_block(i, k))),
                  pl.BlockSpec((MIX_BK, d), lambda i, k: (col_block(i, k), 0)),
                  pl.BlockSpec((MIX_ROWS, d), slab),
                  vec, mod(2), vec, mod(3), mod(4)],
        out_specs=[pl.BlockSpec((MIX_ROWS, d), slab), pl.BlockSpec((MIX_ROWS, d), slab)],
        out_shape=[jax.ShapeDtypeStruct((t, d), F32), jax.ShapeDtypeStruct((t, d), BF16)],
        scratch_shapes=[pltpu.VMEM((MIX_BM, d), F32), pltpu.VMEM((MIX_BM, d), F32)],
        compiler_params=_params("arbitrary", "arbitrary"),
        name="mix_out",
    )(sgu_act, attn_act, gates, gates, wps, wpa, wout, x2, g_post.reshape(1, d), mod3,
      g_pre.reshape(1, d), mod3, mod3)


FFN_BM = 1024
FFN_BH = 256
EPI_ROWS = 32
EPI_SLABS = FFN_BM // EPI_ROWS


def _ffn_kernel(ni, h_ref, wg_ref, wu_ref, wd_ref, x1_ref, gpost_ref, gt_ref, o_ref,
                acc0_ref, acc1_ref):
    i = pl.program_id(0)
    j = pl.program_id(1)
    accs = (acc0_ref, acc1_ref)
    odd = (i % 2) == 1

    def accumulate(acc_ref):
        h = h_ref[...]
        a = jnp.dot(h, wg_ref[...], preferred_element_type=F32)
        b = jnp.dot(h, wu_ref[...], preferred_element_type=F32)
        g = (jax.nn.silu(a) * b).astype(BF16)
        acc_ref[...] += jnp.dot(g, wd_ref[...], preferred_element_type=F32)

    def write_rows(acc_ref):
        r = pl.multiple_of(jnp.minimum(j, EPI_SLABS - 1) * EPI_ROWS, EPI_ROWS)
        f = acc_ref[pl.ds(r, EPI_ROWS), :]
        o_ref[...] = x1_ref[...] + gt_ref[...] * _rms(f, gpost_ref[...])

    for par in (0, 1):
        is_par = odd if par else jnp.logical_not(odd)

        @pl.when((j == 0) & (i < ni) & is_par)
        def _():
            accs[par][...] = jnp.zeros(accs[par].shape, F32)

    @pl.when(i == 0)
    def _():
        accumulate(accs[0])

    for par in (0, 1):
        is_par = odd if par else jnp.logical_not(odd)

        @pl.when((i > 0) & (i < ni) & is_par)
        def _():
            write_rows(accs[1 - par])
            accumulate(accs[par])

    @pl.when(i == ni)
    def _():
        write_rows(accs[(ni - 1) % 2])


def _ffn(h2, wg, wu, wd, x1, g_post, mod3, seq):
    t, d = h2.shape
    hid = wg.shape[1]
    ni, nj = t // FFN_BM, hid // FFN_BH
    bpb = seq // FFN_BM

    def row_block(i):
        return jnp.minimum(i, ni - 1)

    def hid_block(i, j):
        return jnp.where(i == ni, nj - 1, j)

    def slab(i, j):
        return (jnp.where(i == 0, 0, (i - 1) * EPI_SLABS + jnp.minimum(j, EPI_SLABS - 1)), 0)

    def gate(i, j):
        return ((jnp.maximum(i - 1, 0) // bpb) * N_MOD + 5, 0, 0)

    return pl.pallas_call(
        functools.partial(_ffn_kernel, ni),
        grid=(ni + 1, nj),
        in_specs=[pl.BlockSpec((FFN_BM, d), lambda i, j: (row_block(i), 0),
                               pipeline_mode=pl.Buffered(1)),
                  pl.BlockSpec((d, FFN_BH), lambda i, j: (0, hid_block(i, j))),
                  pl.BlockSpec((d, FFN_BH), lambda i, j: (0, hid_block(i, j))),
                  pl.BlockSpec((FFN_BH, d), lambda i, j: (hid_block(i, j), 0)),
                  pl.BlockSpec((EPI_ROWS, d), slab),
                  pl.BlockSpec((1, d), lambda i, j: (0, 0)),
                  pl.BlockSpec((None, 1, d), gate)],
        out_specs=pl.BlockSpec((EPI_ROWS, d), slab),
        out_shape=jax.ShapeDtypeStruct((t, d), F32),
        scratch_shapes=[pltpu.VMEM((FFN_BM, d), F32), pltpu.VMEM((FFN_BM, d), F32)],
        compiler_params=_params("arbitrary", "arbitrary"),
        name="ffn",
    )(h2, wg, wu, wd, x1, g_post.reshape(1, d), mod3)


def _rope_tables(positions):
    half = ROT_DIM // 2
    inv_freq = ROPE_THETA ** (-jnp.arange(0, ROT_DIM, 2, dtype=F32) / ROT_DIM)
    t = positions.size
    per_row = 128 // half
    pos = positions.reshape(t // per_row, per_row).astype(F32)
    ang = (pos[:, :, None] * inv_freq[None, None, :]).reshape(t // per_row, per_row * half)
    cos = jnp.cos(ang).reshape(t, half)
    sin = jnp.sin(ang).reshape(t, half)
    ones = jnp.ones((t, HEAD_DIM - ROT_DIM), F32)
    zeros_h = jnp.zeros((t, half), F32)
    zeros_r = jnp.zeros((t, HEAD_DIM - ROT_DIM), F32)
    c = jnp.concatenate([cos, cos, ones], axis=1)
    s1 = jnp.concatenate([-sin, zeros_h, zeros_r], axis=1)
    s2 = jnp.concatenate([zeros_h, sin, zeros_r], axis=1)
    return c, s1, s2


def kernel(x, c, positions, w_ada, b_ada, g_pre_mix, w_in, attn_sinks, sgu_ln_g, sgu_ln_b, sgu_w, sgu_b, w_proj_sgu, w_proj_attn, w_out, g_post_mix, g_pre_ffn, w_gate, w_up, w_down, g_post_ffn):
    bsz, seq, d = x.shape
    depth = w_ada.shape[0]
    t = bsz * seq
    xcur = x.reshape(t, d)
    rope_c, rope_s1, rope_s2 = _rope_tables(positions)
    c_pad = jnp.pad(c, ((0, 8 - bsz), (0, 0)))
    for l in range(depth):
        mod = _ada_mod(c_pad, w_ada[l], b_ada[l])[:bsz]
        mod3 = mod.reshape(bsz * N_MOD, 1, d)

        h1 = _prenorm(xcur, g_pre_mix[l], mod3, seq, 0, 1)
        qkv, wps_bf, wpa_bf, wout_bf = _in_proj(
            _rope_epilogue, h1, w_in[l], 0, QKV_WIDTH, tables=(rope_c, rope_s1, rope_s2),
            casts=((w_proj_sgu[l], CAST_ROWS_PROJ), (w_proj_attn[l], CAST_ROWS_PROJ),
                   (w_out[l], CAST_ROWS_OUT)),
            extra_scratch=(pltpu.VMEM((PROJ_BM, PROJ_BN), F32),), name="in_proj_qkv")
        uv, wd_bf = _in_proj(functools.partial(_act_epilogue, _gelu), h1, w_in[l], QKV_WIDTH,
                             2 * SGU_WIDTH, casts=((w_down[l], CAST_ROWS_DOWN),),
                             name="in_proj_uv")
        gates, wg_bf, wu_bf = _in_proj(functools.partial(_act_epilogue, lambda z: z), h1,
                                       w_in[l], QKV_WIDTH + 2 * SGU_WIDTH, 2 * d,
                                       casts=((w_gate[l], CAST_ROWS_UP), (w_up[l], CAST_ROWS_UP)),
                                       name="in_proj_gates")

        attn_act = _attention(qkv, attn_sinks[l], seq)
        sgu_act = _sgu(uv, sgu_ln_g[l], sgu_ln_b[l], sgu_w[l], sgu_b[l])
        x1, h2 = _mix(sgu_act, attn_act, gates, wps_bf, wpa_bf, wout_bf, xcur,
                      g_post_mix[l], g_pre_ffn[l], mod3, seq)

        xcur = _ffn(h2, wg_bf, wu_bf, wd_bf, x1, g_post_ffn[l], mod3, seq)
    return xcur.reshape(bsz, seq, d)
```

```python
import functools

import jax
import jax.numpy as jnp
from jax import lax
from jax.experimental import pallas as pl
from jax.experimental.pallas import tpu as pltpu

D_MODEL = 4096
HEAD_DIM = 128
N_Q_HEADS = 16
N_KV_HEADS = 4
Q_PER_KV = 4
ATTN_WIDTH = N_Q_HEADS * HEAD_DIM
KV_WIDTH = N_KV_HEADS * HEAD_DIM
QKV_WIDTH = ATTN_WIDTH + 2 * KV_WIDTH
WINDOW = 128
BLOCK = 128
ROT_DIM = 32
ROPE_THETA = 500000.0
SGU_CHUNK = 128
SGU_GROUPS = 16
SGU_WIDTH = 2048
FFN_HIDDEN = 11008
N_MOD = 6
RMS_EPS = 1e-6
LN_EPS = 1e-5

VMEM_LIMIT_BYTES = 56 * 1024 * 1024

BF16 = jnp.bfloat16
F32 = jnp.float32


def _params(*semantics):
    return pltpu.CompilerParams(dimension_semantics=semantics,
                                vmem_limit_bytes=VMEM_LIMIT_BYTES)


def _rms(v, g):
    return v * lax.rsqrt(jnp.mean(v * v, axis=-1, keepdims=True) + RMS_EPS) * g


ADA_BN = 1024


def _ada_kernel(c_ref, w_ref, b_ref, o_ref):
    c_act = jax.nn.silu(c_ref[...]).astype(BF16)
    o_ref[...] = jnp.dot(c_act, w_ref[...].astype(BF16),
                         preferred_element_type=F32) + b_ref[...]


def _ada_mod(c_pad, w_ada, b_ada):
    rows, d = c_pad.shape
    n = w_ada.shape[1]
    return pl.pallas_call(
        _ada_kernel,
        grid=(n // ADA_BN,),
        in_specs=[pl.BlockSpec((rows, d), lambda j: (0, 0)),
                  pl.BlockSpec((d, ADA_BN), lambda j: (0, j)),
                  pl.BlockSpec((1, ADA_BN), lambda j: (0, j))],
        out_specs=pl.BlockSpec((rows, ADA_BN), lambda j: (0, j)),
        out_shape=jax.ShapeDtypeStruct((rows, n), F32),
        compiler_params=_params("arbitrary"),
        name="ada_mod",
    )(c_pad, w_ada, b_ada.reshape(1, n))


NORM_BM = 512


def _prenorm_kernel(x_ref, g_ref, sh_ref, sc_ref, o_ref):
    y = _rms(x_ref[...], g_ref[...])
    o_ref[...] = (y * (1.0 + sc_ref[...]) + sh_ref[...]).astype(o_ref.dtype)


def _mod_spec(blocks_per_batch, m):
    return pl.BlockSpec((None, 1, D_MODEL),
                        lambda i: ((i // blocks_per_batch) * N_MOD + m, 0, 0))


def _prenorm(x2, g, mod3, seq, m_shift, m_scale):
    t, d = x2.shape
    bpb = seq // NORM_BM
    return pl.pallas_call(
        _prenorm_kernel,
        grid=(t // NORM_BM,),
        in_specs=[pl.BlockSpec((NORM_BM, d), lambda i: (i, 0)),
                  pl.BlockSpec((1, d), lambda i: (0, 0)),
                  _mod_spec(bpb, m_shift), _mod_spec(bpb, m_scale)],
        out_specs=pl.BlockSpec((NORM_BM, d), lambda i: (i, 0)),
        out_shape=jax.ShapeDtypeStruct((t, d), BF16),
        compiler_params=_params("arbitrary"),
        name="prenorm",
    )(x2, g.reshape(1, d), mod3, mod3)


PROJ_BM = 1024
PROJ_BN = 512


def _cast_weight_once(w_ref, wbf_ref):
    @pl.when(pl.program_id(1) == 0)
    def _():
        wbf_ref[...] = w_ref[...].astype(BF16)


ROPE_ROWS = 128
CAST_ROWS_UP = 16
CAST_ROWS_DOWN = 128
CAST_ROWS_PROJ = 32
CAST_ROWS_OUT = 64


def _rope_epilogue(z, o_ref, tables, scratch):
    c_ref, s1_ref, s2_ref = tables
    (z_ref,) = scratch
    z_ref[...] = z
    is_v = pl.program_id(0) >= (ATTN_WIDTH + KV_WIDTH) // PROJ_BN
    half = ROT_DIM // 2

    @pl.when(jnp.logical_not(is_v))
    def _():
        for r in range(0, PROJ_BM, ROPE_ROWS):
            c = c_ref[r:r + ROPE_ROWS, :]
            s1 = s1_ref[r:r + ROPE_ROWS, :]
            s2 = s2_ref[r:r + ROPE_ROWS, :]
            for hd in range(0, PROJ_BN, HEAD_DIM):
                zh = z_ref[r:r + ROPE_ROWS, hd:hd + HEAD_DIM]
                up = pltpu.roll(zh, HEAD_DIM - half, 1)
                down = pltpu.roll(zh, half, 1)
                o_ref[r:r + ROPE_ROWS, hd:hd + HEAD_DIM] = (
                    zh * c + up * s1 + down * s2).astype(o_ref.dtype)

    @pl.when(is_v)
    def _():
        o_ref[...] = z_ref[...].astype(o_ref.dtype)


def _act_epilogue(act, z, o_ref, tables, scratch):
    o_ref[...] = act(z).astype(o_ref.dtype)


def _gelu(z):
    return 0.5 * z * (1.0 + lax.erf(z * (0.5 ** 0.5)))


def _proj_kernel(epilogue, n_tables, n_casts, *refs):
    h_ref, w_ref = refs[:2]
    tables = refs[2:2 + n_tables]
    cast_in = refs[2 + n_tables:2 + n_tables + n_casts]
    o_ref = refs[2 + n_tables + n_casts]
    cast_out = refs[3 + n_tables + n_casts:3 + n_tables + 2 * n_casts]
    wbf_ref = refs[3 + n_tables + 2 * n_casts]
    scratch = refs[4 + n_tables + 2 * n_casts:]
    _cast_weight_once(w_ref, wbf_ref)
    for src, dst in zip(cast_in, cast_out):
        dst[...] = src[...].astype(BF16)
    z = jnp.dot(h_ref[...], wbf_ref[...], preferred_element_type=F32)
    epilogue(z, o_ref, tables, scratch)


def _in_proj(epilogue, h, w_in, col_off, width, tables=(), casts=(), extra_scratch=(), name=None):
    t, d = h.shape
    nj, ni = width // PROJ_BN, t // PROJ_BM
    off_blocks = col_off // PROJ_BN
    table_specs = [pl.BlockSpec((PROJ_BM, a.shape[1]), lambda j, i: (i, 0)) for a in tables]
    cast_specs = []
    for a, rows in casts:
        n_slices = a.shape[0] // rows
        assert n_slices * rows == a.shape[0] and n_slices <= nj * ni, (a.shape, rows)
        cast_specs.append(pl.BlockSpec(
            (rows, a.shape[1]),
            functools.partial(lambda j, i, last: (jnp.minimum(j * ni + i, last), 0), last=n_slices - 1)))
    casts = [a for a, _ in casts]
    return pl.pallas_call(
        functools.partial(_proj_kernel, epilogue, len(tables), len(casts)),
        grid=(nj, ni),
        in_specs=[pl.BlockSpec((PROJ_BM, d), lambda j, i: (i, 0)),
                  pl.BlockSpec((d, PROJ_BN), lambda j, i: (0, j + off_blocks))] + table_specs + cast_specs,
        out_specs=[pl.BlockSpec((PROJ_BM, PROJ_BN), lambda j, i: (i, j))] + cast_specs,
        out_shape=[jax.ShapeDtypeStruct((t, width), BF16)]
        + [jax.ShapeDtypeStruct(a.shape, BF16) for a in casts],
        scratch_shapes=[pltpu.VMEM((d, PROJ_BN), BF16), *extra_scratch],
        compiler_params=_params("arbitrary", "arbitrary"),
        name=name,
    )(h, w_in, *tables, *casts)


ATT_ROWS = 1024


def _attn_kernel(sink_ref, q_ref, kc_ref, kp_ref, vc_ref, vp_ref, o_ref, *, blocks_per_seq):
    r = pl.program_id(0)
    h = pl.program_id(1)
    seq_start = (r % blocks_per_seq) == 0
    qi = lax.broadcasted_iota(jnp.int32, (BLOCK, 2 * BLOCK), 0)
    ki = lax.broadcasted_iota(jnp.int32, (BLOCK, 2 * BLOCK), 1)
    diff = qi + BLOCK - ki
    band = (diff >= 0) & (diff < WINDOW)
    band_first = band & ((ki >= BLOCK) | jnp.logical_not(seq_start))
    sink_col = ki == 0
    v_row = lax.broadcasted_iota(jnp.int32, (2 * BLOCK, HEAD_DIM), 0)
    ones = jnp.ones((2 * BLOCK, HEAD_DIM), BF16)
    scale = HEAD_DIM ** -0.5
    for n in range(ATT_ROWS // BLOCK):
        lo = n * BLOCK
        q4 = jnp.concatenate(
            [q_ref[lo:lo + BLOCK, g * HEAD_DIM:(g + 1) * HEAD_DIM] for g in range(Q_PER_KV)], axis=0)
        if n == 0:
            kk = jnp.concatenate([kp_ref[...], kc_ref[0:BLOCK, :]], axis=0)
            vv = jnp.concatenate([vp_ref[...], vc_ref[0:BLOCK, :]], axis=0)
            valid = band_first
        else:
            kk = kc_ref[lo - BLOCK:lo + BLOCK, :]
            vv = vc_ref[lo - BLOCK:lo + BLOCK, :]
            valid = band
        v_ones = jnp.concatenate([jnp.where(v_row == 0, jnp.zeros_like(vv), vv), ones], axis=1)
        s = lax.dot_general(q4, kk, (((1,), (1,)), ((), ())), preferred_element_type=F32)
        p = []
        for g in range(Q_PER_KV):
            sg = s[g * BLOCK:(g + 1) * BLOCK, :] * scale
            sg = jnp.where(sink_col, sink_ref[h * Q_PER_KV + g], jnp.where(valid, sg, -jnp.inf))
            m = jnp.max(sg, axis=-1, keepdims=True)
            p.append(jnp.exp(sg - m).astype(BF16))
        o = jnp.dot(jnp.concatenate(p, axis=0), v_ones, preferred_element_type=F32)
        o = o[:, :HEAD_DIM] / o[:, HEAD_DIM:]
        for g in range(Q_PER_KV):
            o_ref[lo:lo + BLOCK, g * HEAD_DIM:(g + 1) * HEAD_DIM] = (
                o[g * BLOCK:(g + 1) * BLOCK, :].astype(o_ref.dtype))


def _attention(qkv, sinks, seq):
    t = qkv.shape[0]
    bps = seq // ATT_ROWS
    qw = Q_PER_KV * HEAD_DIM
    k_col = ATTN_WIDTH // HEAD_DIM
    v_col = (ATTN_WIDTH + KV_WIDTH) // HEAD_DIM
    sub = ATT_ROWS // BLOCK

    def prev(r):
        return jnp.maximum(r * sub - 1, 0)

    return pl.pallas_call(
        functools.partial(_attn_kernel, blocks_per_seq=bps),
        grid=(t // ATT_ROWS, N_KV_HEADS),
        in_specs=[pl.BlockSpec(memory_space=pltpu.SMEM),
                  pl.BlockSpec((ATT_ROWS, qw), lambda r, h: (r, h)),
                  pl.BlockSpec((ATT_ROWS, HEAD_DIM), lambda r, h: (r, k_col + h)),
                  pl.BlockSpec((BLOCK, HEAD_DIM), lambda r, h: (prev(r), k_col + h)),
                  pl.BlockSpec((ATT_ROWS, HEAD_DIM), lambda r, h: (r, v_col + h)),
                  pl.BlockSpec((BLOCK, HEAD_DIM), lambda r, h: (prev(r), v_col + h))],
        out_specs=pl.BlockSpec((ATT_ROWS, qw), lambda r, h: (r, h)),
        out_shape=jax.ShapeDtypeStruct((t, ATTN_WIDTH), BF16),
        compiler_params=_params("arbitrary", "arbitrary"),
        name="swa_attention",
    )(sinks, qkv, qkv, qkv, qkv, qkv)


SGU_ROWS = 512


def _sgu_kernel(u_ref, v_ref, lng_ref, lnb_ref, w_ref, b_ref, o_ref):
    ti = lax.broadcasted_iota(jnp.int32, (SGU_CHUNK, SGU_CHUNK), 0)
    si = lax.broadcasted_iota(jnp.int32, (SGU_CHUNK, SGU_CHUNK), 1)
    causal = si <= ti
    gw = SGU_WIDTH // SGU_GROUPS
    for c in range(SGU_ROWS // SGU_CHUNK):
        lo = c * SGU_CHUNK
        v = v_ref[lo:lo + SGU_CHUNK, :].astype(F32)
        mu = jnp.mean(v, axis=-1, keepdims=True)
        vc = v - mu
        vn = vc * lax.rsqrt(jnp.mean(vc * vc, axis=-1, keepdims=True) + LN_EPS)
        vn = (vn * lng_ref[...] + lnb_ref[...]).astype(BF16)
        for g in range(SGU_GROUPS):
            w = jnp.where(causal, w_ref[g], 0.0).astype(BF16)
            mixed = jnp.dot(w, vn[:, g * gw:(g + 1) * gw], preferred_element_type=F32) + b_ref[g]
            u = u_ref[lo:lo + SGU_CHUNK, g * gw:(g + 1) * gw].astype(F32)
            o_ref[lo:lo + SGU_CHUNK, g * gw:(g + 1) * gw] = (u * mixed).astype(o_ref.dtype)


def _sgu(uv, ln_g, ln_b, w_s, b_s):
    t = uv.shape[0]
    b_bcast = jnp.broadcast_to(b_s[:, :, None], (SGU_GROUPS, SGU_CHUNK, SGU_WIDTH // SGU_GROUPS))
    full3 = lambda i: (0, 0, 0)
    return pl.pallas_call(
        _sgu_kernel,
        grid=(t // SGU_ROWS,),
        in_specs=[pl.BlockSpec((SGU_ROWS, SGU_WIDTH), lambda i: (i, 0)),
                  pl.BlockSpec((SGU_ROWS, SGU_WIDTH), lambda i: (i, 1)),
                  pl.BlockSpec((1, SGU_WIDTH), lambda i: (0, 0)),
                  pl.BlockSpec((1, SGU_WIDTH), lambda i: (0, 0)),
                  pl.BlockSpec(w_s.shape, full3),
                  pl.BlockSpec(b_bcast.shape, full3)],
        out_specs=pl.BlockSpec((SGU_ROWS, SGU_WIDTH), lambda i: (i, 0)),
        out_shape=jax.ShapeDtypeStruct((t, SGU_WIDTH), BF16),
        compiler_params=_params("arbitrary"),
        name="sgu",
    )(uv, uv, ln_g.reshape(1, -1), ln_b.reshape(1, -1), w_s, b_bcast)


MIX_BM = 512
MIX_BK = 512
MIX_ROWS = MIX_BM // (D_MODEL // MIX_BK)


def _mix_kernel(ni, a_ref, b_ref, ga_ref, gb_ref, wa_ref, wb_ref, wo_ref, x_ref, gpost_ref,
                gt_ref, gpre_ref, sh_ref, sc_ref, x1_ref, h2_ref, acc0_ref, acc1_ref):
    i = pl.program_id(0)
    k = pl.program_id(1)
    accs = (acc0_ref, acc1_ref)
    odd = (i % 2) == 1

    def accumulate(acc_ref):
        ya = jnp.dot(a_ref[...], wa_ref[...], preferred_element_type=F32)
        yb = jnp.dot(b_ref[...], wb_ref[...], preferred_element_type=F32)
        ga = jax.nn.sigmoid(ga_ref[...].astype(F32))
        gb = jax.nn.sigmoid(gb_ref[...].astype(F32))
        merged = (ga * ya + gb * yb).astype(BF16)
        acc_ref[...] += jnp.dot(merged, wo_ref[...], preferred_element_type=F32)

    def write_rows(acc_ref):
        r = pl.multiple_of(k * MIX_ROWS, MIX_ROWS)
        tproj = acc_ref[pl.ds(r, MIX_ROWS), :]
        x1 = x_ref[...] + gt_ref[...] * _rms(tproj, gpost_ref[...])
        x1_ref[...] = x1
        h2_ref[...] = (_rms(x1, gpre_ref[...]) * (1.0 + sc_ref[...]) + sh_ref[...]).astype(h2_ref.dtype)

    for par in (0, 1):
        is_par = odd if par else jnp.logical_not(odd)

        @pl.when((k == 0) & (i < ni) & is_par)
        def _():
            accs[par][...] = jnp.zeros(accs[par].shape, F32)

    @pl.when(i == 0)
    def _():
        accumulate(accs[0])

    for par in (0, 1):
        is_par = odd if par else jnp.logical_not(odd)

        @pl.when((i > 0) & (i < ni) & is_par)
        def _():
            write_rows(accs[1 - par])
            accumulate(accs[par])

    @pl.when(i == ni)
    def _():
        write_rows(accs[(ni - 1) % 2])


def _mix(sgu_act, attn_act, gates, wps, wpa, wout, x2, g_post, g_pre, mod3, seq):
    t, ka = sgu_act.shape
    d = wout.shape[1]
    ni, nk = t // MIX_BM, d // MIX_BK
    bpb = seq // MIX_BM

    def row_block(i):
        return jnp.minimum(i, ni - 1)

    def col_block(i, k):
        return jnp.where(i == ni, nk - 1, k)

    def slab(i, k):
        return (jnp.where(i == 0, 0, (i - 1) * nk + k), 0)

    def mod(m):
        return pl.BlockSpec((None, 1, d),
                            lambda i, k: ((jnp.maximum(i - 1, 0) // bpb) * N_MOD + m, 0, 0))

    vec = pl.BlockSpec((1, d), lambda i, k: (0, 0))
    return pl.pallas_call(
        functools.partial(_mix_kernel, ni),
        grid=(ni + 1, nk),
        in_specs=[pl.BlockSpec((MIX_BM, ka), lambda i, k: (row_block(i), 0)),
                  pl.BlockSpec((MIX_BM, ka), lambda i, k: (row_block(i), 0)),
                  pl.BlockSpec((MIX_BM, MIX_BK), lambda i, k: (row_block(i), col_block(i, k))),
                  pl.BlockSpec((MIX_BM, MIX_BK), lambda i, k: (row_block(i), col_block(i, k) + nk)),
                  pl.BlockSpec((ka, MIX_BK), lambda i, k: (0, col_block(i, k))),
                  pl.BlockSpec((ka, MIX_BK), lambda i, k: (0, col_block(i, k))),
                  pl.BlockSpec((MIX_BK, d), lambda i, k: (col_block(i, k), 0)),
                  pl.BlockSpec((MIX_ROWS, d), slab),
                  vec, mod(2), vec, mod(3), mod(4)],
        out_specs=[pl.BlockSpec((MIX_ROWS, d), slab), pl.BlockSpec((MIX_ROWS, d), slab)],
        out_shape=[jax.ShapeDtypeStruct((t, d), F32), jax.ShapeDtypeStruct((t, d), BF16)],
        scratch_shapes=[pltpu.VMEM((MIX_BM, d), F32), pltpu.VMEM((MIX_BM, d), F32)],
        compiler_params=_params("arbitrary", "arbitrary"),
        name="mix_out",
    )(sgu_act, attn_act, gates, gates, wps, wpa, wout, x2, g_post.reshape(1, d), mod3,
      g_pre.reshape(1, d), mod3, mod3)


FFN_BM = 1024
FFN_BH = 256
EPI_ROWS = 32
EPI_SLABS = FFN_BM // EPI_ROWS


def _ffn_kernel(ni, acc0_ref, acc1_ref, step, h_ref, wg_ref, wu_ref, wd_ref, x1_ref, gpost_ref,
                gt_ref, o_ref):
    i, j = step if isinstance(step, tuple) else step.index
    accs = (acc0_ref, acc1_ref)
    odd = (i % 2) == 1

    def accumulate(acc_ref):
        h = h_ref[...]
        a = jnp.dot(h, wg_ref[...], preferred_element_type=F32)
        b = jnp.dot(h, wu_ref[...], preferred_element_type=F32)
        g = (jax.nn.silu(a) * b).astype(BF16)
        acc_ref[...] += jnp.dot(g, wd_ref[...], preferred_element_type=F32)

    def write_rows(acc_ref):
        r = pl.multiple_of(jnp.minimum(j, EPI_SLABS - 1) * EPI_ROWS, EPI_ROWS)
        f = acc_ref[pl.ds(r, EPI_ROWS), :]
        o_ref[...] = x1_ref[...] + gt_ref[...] * _rms(f, gpost_ref[...])

    for par in (0, 1):
        is_par = odd if par else jnp.logical_not(odd)

        @pl.when((j == 0) & (i < ni) & is_par)
        def _():
            accs[par][...] = jnp.zeros(accs[par].shape, F32)

    @pl.when(i == 0)
    def _():
        accumulate(accs[0])

    for par in (0, 1):
        is_par = odd if par else jnp.logical_not(odd)

        @pl.when((i > 0) & (i < ni) & is_par)
        def _():
            write_rows(accs[1 - par])
            accumulate(accs[par])

    @pl.when(i == ni)
    def _():
        write_rows(accs[(ni - 1) % 2])


def _ffn(h2, wg, wu, wd, x1, g_post, mod3, seq):
    t, d = h2.shape
    hid = wg.shape[1]
    ni, nj = t // FFN_BM, hid // FFN_BH
    bpb = seq // FFN_BM

    def row_block(i):
        return jnp.minimum(i, ni - 1)

    def hid_block(i, j):
        return jnp.where(i == ni, nj - 1, j)

    def slab(i, j):
        return (jnp.where(i == 0, 0, (i - 1) * EPI_SLABS + jnp.minimum(j, EPI_SLABS - 1)), 0)

    def gate(i, j):
        return ((jnp.maximum(i - 1, 0) // bpb) * N_MOD + 5, 0, 0)

    in_specs = [pl.BlockSpec((FFN_BM, d), lambda i, j: (row_block(i), 0),
                             pipeline_mode=pl.Buffered(1)),
                pl.BlockSpec((d, FFN_BH), lambda i, j: (0, hid_block(i, j))),
                pl.BlockSpec((d, FFN_BH), lambda i, j: (0, hid_block(i, j))),
                pl.BlockSpec((FFN_BH, d), lambda i, j: (hid_block(i, j), 0)),
                pl.BlockSpec((EPI_ROWS, d), slab),
                pl.BlockSpec((1, d), lambda i, j: (0, 0)),
                pl.BlockSpec((None, 1, d), gate)]
    out_specs = [pl.BlockSpec((EPI_ROWS, d), slab)]

    def whole_ffn(*refs):
        *operands, acc0_ref, acc1_ref = refs
        pltpu.emit_pipeline(
            functools.partial(_ffn_kernel, ni, acc0_ref, acc1_ref),
            grid=(ni + 1, nj), in_specs=in_specs, out_specs=out_specs, _explicit_indices=True,
        )(*operands)

    hbm = pl.BlockSpec(memory_space=pl.ANY)
    return pl.pallas_call(
        whole_ffn,
        in_specs=[hbm] * len(in_specs),
        out_specs=hbm,
        out_shape=jax.ShapeDtypeStruct((t, d), F32),
        scratch_shapes=[pltpu.VMEM((FFN_BM, d), F32), pltpu.VMEM((FFN_BM, d), F32)],
        compiler_params=pltpu.CompilerParams(vmem_limit_bytes=VMEM_LIMIT_BYTES),
        name="ffn",
    )(h2, wg, wu, wd, x1, g_post.reshape(1, d), mod3)


def _rope_tables(positions):
    half = ROT_DIM // 2
    inv_freq = ROPE_THETA ** (-jnp.arange(0, ROT_DIM, 2, dtype=F32) / ROT_DIM)
    ang = positions.reshape(-1).astype(F32)[:, None] * inv_freq
    cos, sin = jnp.cos(ang), jnp.sin(ang)
    t = ang.shape[0]
    ones = jnp.ones((t, HEAD_DIM - ROT_DIM), F32)
    zeros_h = jnp.zeros((t, half), F32)
    zeros_r = jnp.zeros((t, HEAD_DIM - ROT_DIM), F32)
    c = jnp.concatenate([cos, cos, ones], axis=1)
    s1 = jnp.concatenate([-sin, zeros_h, zeros_r], axis=1)
    s2 = jnp.concatenate([zeros_h, sin, zeros_r], axis=1)
    return c, s1, s2


def kernel(x, c, positions, w_ada, b_ada, g_pre_mix, w_in, attn_sinks, sgu_ln_g, sgu_ln_b, sgu_w, sgu_b, w_proj_sgu, w_proj_attn, w_out, g_post_mix, g_pre_ffn, w_gate, w_up, w_down, g_post_ffn):
    bsz, seq, d = x.shape
    depth = w_ada.shape[0]
    t = bsz * seq
    xcur = x.reshape(t, d)
    rope_c, rope_s1, rope_s2 = _rope_tables(positions)
    c_pad = jnp.pad(c, ((0, 8 - bsz), (0, 0)))
    for l in range(depth):
        mod = _ada_mod(c_pad, w_ada[l], b_ada[l])[:bsz]
        mod3 = mod.reshape(bsz * N_MOD, 1, d)

        h1 = _prenorm(xcur, g_pre_mix[l], mod3, seq, 0, 1)
        qkv, wps_bf, wpa_bf, wout_bf = _in_proj(
            _rope_epilogue, h1, w_in[l], 0, QKV_WIDTH, tables=(rope_c, rope_s1, rope_s2),
            casts=((w_proj_sgu[l], CAST_ROWS_PROJ), (w_proj_attn[l], CAST_ROWS_PROJ),
                   (w_out[l], CAST_ROWS_OUT)),
            extra_scratch=(pltpu.VMEM((PROJ_BM, PROJ_BN), F32),), name="in_proj_qkv")
        uv, wd_bf = _in_proj(functools.partial(_act_epilogue, _gelu), h1, w_in[l], QKV_WIDTH,
                             2 * SGU_WIDTH, casts=((w_down[l], CAST_ROWS_DOWN),),
                             name="in_proj_uv")
        gates, wg_bf, wu_bf = _in_proj(functools.partial(_act_epilogue, lambda z: z), h1,
                                       w_in[l], QKV_WIDTH + 2 * SGU_WIDTH, 2 * d,
                                       casts=((w_gate[l], CAST_ROWS_UP), (w_up[l], CAST_ROWS_UP)),
                                       name="in_proj_gates")

        attn_act = _attention(qkv, attn_sinks[l], seq)
        sgu_act = _sgu(uv, sgu_ln_g[l], sgu_ln_b[l], sgu_w[l], sgu_b[l])
        x1, h2 = _mix(sgu_act, attn_act, gates, wps_bf, wpa_bf, wout_bf, xcur,
                      g_post_mix[l], g_pre_ffn[l], mod3, seq)

        xcur = _ffn(h2, wg_bf, wu_bf, wd_bf, x1, g_post_ffn[l], mod3, seq)
    return xcur.reshape(bsz, seq, d)
```

```python
import functools

import jax
import jax.numpy as jnp
from jax import lax
from jax.experimental import pallas as pl
from jax.experimental.pallas import tpu as pltpu

D_MODEL = 4096
HEAD_DIM = 128
N_Q_HEADS = 16
N_KV_HEADS = 4
Q_PER_KV = 4
ATTN_WIDTH = N_Q_HEADS * HEAD_DIM
KV_WIDTH = N_KV_HEADS * HEAD_DIM
QKV_WIDTH = ATTN_WIDTH + 2 * KV_WIDTH
WINDOW = 128
BLOCK = 128
ROT_DIM = 32
ROPE_THETA = 500000.0
SGU_CHUNK = 128
SGU_GROUPS = 16
SGU_WIDTH = 2048
FFN_HIDDEN = 11008
N_MOD = 6
RMS_EPS = 1e-6
LN_EPS = 1e-5

VMEM_LIMIT_BYTES = 56 * 1024 * 1024

BF16 = jnp.bfloat16
F32 = jnp.float32


def _params(*semantics):
    return pltpu.CompilerParams(dimension_semantics=semantics,
                                vmem_limit_bytes=VMEM_LIMIT_BYTES)


def _rms(v, g):
    return v * lax.rsqrt(jnp.mean(v * v, axis=-1, keepdims=True) + RMS_EPS) * g


ADA_BN = 1024


def _ada_kernel(c_ref, w_ref, b_ref, o_ref):
    c_act = jax.nn.silu(c_ref[...]).astype(BF16)
    o_ref[...] = jnp.dot(c_act, w_ref[...].astype(BF16),
                         preferred_element_type=F32) + b_ref[...]


def _ada_mod(c_pad, w_ada, b_ada):
    rows, d = c_pad.shape
    n = w_ada.shape[1]
    return pl.pallas_call(
        _ada_kernel,
        grid=(n // ADA_BN,),
        in_specs=[pl.BlockSpec((rows, d), lambda j: (0, 0)),
                  pl.BlockSpec((d, ADA_BN), lambda j: (0, j)),
                  pl.BlockSpec((1, ADA_BN), lambda j: (0, j))],
        out_specs=pl.BlockSpec((rows, ADA_BN), lambda j: (0, j)),
        out_shape=jax.ShapeDtypeStruct((rows, n), F32),
        compiler_params=_params("arbitrary"),
        name="ada_mod",
    )(c_pad, w_ada, b_ada.reshape(1, n))


NORM_BM = 512


def _prenorm_kernel(x_ref, g_ref, sh_ref, sc_ref, o_ref):
    y = _rms(x_ref[...], g_ref[...])
    o_ref[...] = (y * (1.0 + sc_ref[...]) + sh_ref[...]).astype(o_ref.dtype)


def _mod_spec(blocks_per_batch, m):
    return pl.BlockSpec((None, 1, D_MODEL),
                        lambda i: ((i // blocks_per_batch) * N_MOD + m, 0, 0))


def _prenorm(x2, g, mod3, seq, m_shift, m_scale):
    t, d = x2.shape
    bpb = seq // NORM_BM
    return pl.pallas_call(
        _prenorm_kernel,
        grid=(t // NORM_BM,),
        in_specs=[pl.BlockSpec((NORM_BM, d), lambda i: (i, 0)),
                  pl.BlockSpec((1, d), lambda i: (0, 0)),
                  _mod_spec(bpb, m_shift), _mod_spec(bpb, m_scale)],
        out_specs=pl.BlockSpec((NORM_BM, d), lambda i: (i, 0)),
        out_shape=jax.ShapeDtypeStruct((t, d), BF16),
        compiler_params=_params("arbitrary"),
        name="prenorm",
    )(x2, g.reshape(1, d), mod3, mod3)


PROJ_BM = 1024
PROJ_BN = 512


def _cast_weight_once(w_ref, wbf_ref):
    @pl.when(pl.program_id(1) == 0)
    def _():
        wbf_ref[...] = w_ref[...].astype(BF16)


ROPE_ROWS = 128
CAST_ROWS_UP = 16
CAST_ROWS_DOWN = 128
CAST_ROWS_PROJ = 32
CAST_ROWS_OUT = 64


def _rope_epilogue(z, o_ref, tables, scratch):
    c_ref, s1_ref, s2_ref = tables
    (z_ref,) = scratch
    z_ref[...] = z
    is_v = pl.program_id(0) >= (ATTN_WIDTH + KV_WIDTH) // PROJ_BN
    half = ROT_DIM // 2

    @pl.when(jnp.logical_not(is_v))
    def _():
        for r in range(0, PROJ_BM, ROPE_ROWS):
            c = c_ref[r:r + ROPE_ROWS, :]
            s1 = s1_ref[r:r + ROPE_ROWS, :]
            s2 = s2_ref[r:r + ROPE_ROWS, :]
            for hd in range(0, PROJ_BN, HEAD_DIM):
                zh = z_ref[r:r + ROPE_ROWS, hd:hd + HEAD_DIM]
                packed = pltpu.bitcast(zh.astype(BF16), jnp.uint32)
                up = pltpu.bitcast(pltpu.roll(packed, HEAD_DIM - half, 1), BF16).astype(F32)
                down = pltpu.bitcast(pltpu.roll(packed, half, 1), BF16).astype(F32)
                o_ref[r:r + ROPE_ROWS, hd:hd + HEAD_DIM] = (
                    zh * c + up * s1 + down * s2).astype(o_ref.dtype)

    @pl.when(is_v)
    def _():
        o_ref[...] = z_ref[...].astype(o_ref.dtype)


def _act_epilogue(act, z, o_ref, tables, scratch):
    o_ref[...] = act(z).astype(o_ref.dtype)


def _gelu(z):
    return 0.5 * z * (1.0 + lax.erf(z * (0.5 ** 0.5)))


def _proj_kernel(epilogue, n_tables, n_casts, *refs):
    h_ref, w_ref = refs[:2]
    tables = refs[2:2 + n_tables]
    cast_in = refs[2 + n_tables:2 + n_tables + n_casts]
    o_ref = refs[2 + n_tables + n_casts]
    cast_out = refs[3 + n_tables + n_casts:3 + n_tables + 2 * n_casts]
    wbf_ref = refs[3 + n_tables + 2 * n_casts]
    scratch = refs[4 + n_tables + 2 * n_casts:]
    _cast_weight_once(w_ref, wbf_ref)
    for src, dst in zip(cast_in, cast_out):
        dst[...] = src[...].astype(BF16)
    z = jnp.dot(h_ref[...], wbf_ref[...], preferred_element_type=F32)
    epilogue(z, o_ref, tables, scratch)


def _in_proj(epilogue, h, w_in, col_off, width, tables=(), casts=(), extra_scratch=(), name=None):
    t, d = h.shape
    nj, ni = width // PROJ_BN, t // PROJ_BM
    off_blocks = col_off // PROJ_BN
    table_specs = [pl.BlockSpec((PROJ_BM, a.shape[1]), lambda j, i: (i, 0)) for a in tables]
    cast_specs = []
    for a, rows in casts:
        n_slices = a.shape[0] // rows
        assert n_slices * rows == a.shape[0] and n_slices <= nj * ni, (a.shape, rows)
        cast_specs.append(pl.BlockSpec(
            (rows, a.shape[1]),
            functools.partial(lambda j, i, last: (jnp.minimum(j * ni + i, last), 0), last=n_slices - 1)))
    casts = [a for a, _ in casts]
    return pl.pallas_call(
        functools.partial(_proj_kernel, epilogue, len(tables), len(casts)),
        grid=(nj, ni),
        in_specs=[pl.BlockSpec((PROJ_BM, d), lambda j, i: (i, 0)),
                  pl.BlockSpec((d, PROJ_BN), lambda j, i: (0, j + off_blocks))] + table_specs + cast_specs,
        out_specs=[pl.BlockSpec((PROJ_BM, PROJ_BN), lambda j, i: (i, j))] + cast_specs,
        out_shape=[jax.ShapeDtypeStruct((t, width), BF16)]
        + [jax.ShapeDtypeStruct(a.shape, BF16) for a in casts],
        scratch_shapes=[pltpu.VMEM((d, PROJ_BN), BF16), *extra_scratch],
        compiler_params=_params("arbitrary", "arbitrary"),
        name=name,
    )(h, w_in, *tables, *casts)


ATT_ROWS = 1024


def _attn_kernel(sink_ref, q_ref, kc_ref, kp_ref, vc_ref, vp_ref, o_ref, *, blocks_per_seq):
    r = pl.program_id(0)
    h = pl.program_id(1)
    seq_start = (r % blocks_per_seq) == 0
    qi = lax.broadcasted_iota(jnp.int32, (BLOCK, 2 * BLOCK), 0)
    ki = lax.broadcasted_iota(jnp.int32, (BLOCK, 2 * BLOCK), 1)
    diff = qi + BLOCK - ki
    band = (diff >= 0) & (diff < WINDOW)
    band_first = band & ((ki >= BLOCK) | jnp.logical_not(seq_start))
    sink_col = ki == 0
    v_row = lax.broadcasted_iota(jnp.int32, (2 * BLOCK, HEAD_DIM), 0)
    ones = jnp.ones((2 * BLOCK, HEAD_DIM), BF16)
    scale = HEAD_DIM ** -0.5
    for n in range(ATT_ROWS // BLOCK):
        lo = n * BLOCK
        q4 = jnp.concatenate(
            [q_ref[lo:lo + BLOCK, g * HEAD_DIM:(g + 1) * HEAD_DIM] for g in range(Q_PER_KV)], axis=0)
        if n == 0:
            kk = jnp.concatenate([kp_ref[...], kc_ref[0:BLOCK, :]], axis=0)
            vv = jnp.concatenate([vp_ref[...], vc_ref[0:BLOCK, :]], axis=0)
            valid = band_first
        else:
            kk = kc_ref[lo - BLOCK:lo + BLOCK, :]
            vv = vc_ref[lo - BLOCK:lo + BLOCK, :]
            valid = band
        v_ones = jnp.concatenate([jnp.where(v_row == 0, jnp.zeros_like(vv), vv), ones], axis=1)
        s = lax.dot_general(q4, kk, (((1,), (1,)), ((), ())), preferred_element_type=F32)
        p = []
        for g in range(Q_PER_KV):
            sg = s[g * BLOCK:(g + 1) * BLOCK, :] * scale
            sg = jnp.where(sink_col, sink_ref[h * Q_PER_KV + g], jnp.where(valid, sg, -jnp.inf))
            m = jnp.max(sg, axis=-1, keepdims=True)
            p.append(jnp.exp(sg - m).astype(BF16))
        o = jnp.dot(jnp.concatenate(p, axis=0), v_ones, preferred_element_type=F32)
        o = o[:, :HEAD_DIM] / o[:, HEAD_DIM:]
        for g in range(Q_PER_KV):
            o_ref[lo:lo + BLOCK, g * HEAD_DIM:(g + 1) * HEAD_DIM] = (
                o[g * BLOCK:(g + 1) * BLOCK, :].astype(o_ref.dtype))


def _attention(qkv, sinks, seq):
    t = qkv.shape[0]
    bps = seq // ATT_ROWS
    qw = Q_PER_KV * HEAD_DIM
    k_col = ATTN_WIDTH // HEAD_DIM
    v_col = (ATTN_WIDTH + KV_WIDTH) // HEAD_DIM
    sub = ATT_ROWS // BLOCK

    def prev(r):
        return jnp.maximum(r * sub - 1, 0)

    return pl.pallas_call(
        functools.partial(_attn_kernel, blocks_per_seq=bps),
        grid=(t // ATT_ROWS, N_KV_HEADS),
        in_specs=[pl.BlockSpec(memory_space=pltpu.SMEM),
                  pl.BlockSpec((ATT_ROWS, qw), lambda r, h: (r, h)),
                  pl.BlockSpec((ATT_ROWS, HEAD_DIM), lambda r, h: (r, k_col + h)),
                  pl.BlockSpec((BLOCK, HEAD_DIM), lambda r, h: (prev(r), k_col + h)),
                  pl.BlockSpec((ATT_ROWS, HEAD_DIM), lambda r, h: (r, v_col + h)),
                  pl.BlockSpec((BLOCK, HEAD_DIM), lambda r, h: (prev(r), v_col + h))],
        out_specs=pl.BlockSpec((ATT_ROWS, qw), lambda r, h: (r, h)),
        out_shape=jax.ShapeDtypeStruct((t, ATTN_WIDTH), BF16),
        compiler_params=_params("arbitrary", "arbitrary"),
        name="swa_attention",
    )(sinks, qkv, qkv, qkv, qkv, qkv)


SGU_ROWS = 512


def _sgu_kernel(u_ref, v_ref, lng_ref, lnb_ref, w_ref, b_ref, o_ref):
    ti = lax.broadcasted_iota(jnp.int32, (SGU_CHUNK, SGU_CHUNK), 0)
    si = lax.broadcasted_iota(jnp.int32, (SGU_CHUNK, SGU_CHUNK), 1)
    causal = si <= ti
    gw = SGU_WIDTH // SGU_GROUPS
    for c in range(SGU_ROWS // SGU_CHUNK):
        lo = c * SGU_CHUNK
        v = v_ref[lo:lo + SGU_CHUNK, :].astype(F32)
        mu = jnp.mean(v, axis=-1, keepdims=True)
        vc = v - mu
        vn = vc * lax.rsqrt(jnp.mean(vc * vc, axis=-1, keepdims=True) + LN_EPS)
        vn = (vn * lng_ref[...] + lnb_ref[...]).astype(BF16)
        for g in range(SGU_GROUPS):
            w = jnp.where(causal, w_ref[g], 0.0).astype(BF16)
            mixed = jnp.dot(w, vn[:, g * gw:(g + 1) * gw], preferred_element_type=F32) + b_ref[g]
            u = u_ref[lo:lo + SGU_CHUNK, g * gw:(g + 1) * gw].astype(F32)
            o_ref[lo:lo + SGU_CHUNK, g * gw:(g + 1) * gw] = (u * mixed).astype(o_ref.dtype)


def _sgu(uv, ln_g, ln_b, w_s, b_s):
    t = uv.shape[0]
    b_bcast = jnp.broadcast_to(b_s[:, :, None], (SGU_GROUPS, SGU_CHUNK, SGU_WIDTH // SGU_GROUPS))
    full3 = lambda i: (0, 0, 0)
    return pl.pallas_call(
        _sgu_kernel,
        grid=(t // SGU_ROWS,),
        in_specs=[pl.BlockSpec((SGU_ROWS, SGU_WIDTH), lambda i: (i, 0)),
                  pl.BlockSpec((SGU_ROWS, SGU_WIDTH), lambda i: (i, 1)),
                  pl.BlockSpec((1, SGU_WIDTH), lambda i: (0, 0)),
                  pl.BlockSpec((1, SGU_WIDTH), lambda i: (0, 0)),
                  pl.BlockSpec(w_s.shape, full3),
                  pl.BlockSpec(b_bcast.shape, full3)],
        out_specs=pl.BlockSpec((SGU_ROWS, SGU_WIDTH), lambda i: (i, 0)),
        out_shape=jax.ShapeDtypeStruct((t, SGU_WIDTH), BF16),
        compiler_params=_params("arbitrary"),
        name="sgu",
    )(uv, uv, ln_g.reshape(1, -1), ln_b.reshape(1, -1), w_s, b_bcast)


MIX_BM = 512
MIX_BK = 512
MIX_ROWS = MIX_BM // (D_MODEL // MIX_BK)


def _mix_kernel(ni, a_ref, b_ref, ga_ref, gb_ref, wa_ref, wb_ref, wo_ref, x_ref, gpost_ref,
                gt_ref, gpre_ref, sh_ref, sc_ref, x1_ref, h2_ref, acc0_ref, acc1_ref):
    i = pl.program_id(0)
    k = pl.program_id(1)
    accs = (acc0_ref, acc1_ref)
    odd = (i % 2) == 1

    def accumulate(acc_ref):
        ya = jnp.dot(a_ref[...], wa_ref[...], preferred_element_type=F32)
        yb = jnp.dot(b_ref[...], wb_ref[...], preferred_element_type=F32)
        ga = jax.nn.sigmoid(ga_ref[...].astype(F32))
        gb = jax.nn.sigmoid(gb_ref[...].astype(F32))
        merged = (ga * ya + gb * yb).astype(BF16)
        acc_ref[...] += jnp.dot(merged, wo_ref[...], preferred_element_type=F32)

    def write_rows(acc_ref):
        r = pl.multiple_of(k * MIX_ROWS, MIX_ROWS)
        tproj = acc_ref[pl.ds(r, MIX_ROWS), :]
        x1 = x_ref[...] + _rms(tproj, gt_ref[...] * gpost_ref[...])
        x1_ref[...] = x1
        h2 = _rms(x1, gpre_ref[...] * (1.0 + sc_ref[...])) + sh_ref[...]
        h2_ref[...] = h2.astype(h2_ref.dtype)

    for par in (0, 1):
        is_par = odd if par else jnp.logical_not(odd)

        @pl.when((k == 0) & (i < ni) & is_par)
        def _():
            accs[par][...] = jnp.zeros(accs[par].shape, F32)

    @pl.when(i == 0)
    def _():
        accumulate(accs[0])

    for par in (0, 1):
        is_par = odd if par else jnp.logical_not(odd)

        @pl.when((i > 0) & (i < ni) & is_par)
        def _():
            write_rows(accs[1 - par])
            accumulate(accs[par])

    @pl.when(i == ni)
    def _():
        write_rows(accs[(ni - 1) % 2])


def _mix(sgu_act, attn_act, gates, wps, wpa, wout, x2, g_post, g_pre, mod3, seq):
    t, ka = sgu_act.shape
    d = wout.shape[1]
    ni, nk = t // MIX_BM, d // MIX_BK
    bpb = seq // MIX_BM

    def row_block(i):
        return jnp.minimum(i, ni - 1)

    def col_block(i, k):
        return jnp.where(i == ni, nk - 1, k)

    def slab(i, k):
        return (jnp.where(i == 0, 0, (i - 1) * nk + k), 0)

    def mod(m):
        return pl.BlockSpec((None, 1, d),
                            lambda i, k: ((jnp.maximum(i - 1, 0) // bpb) * N_MOD + m, 0, 0))

    vec = pl.BlockSpec((1, d), lambda i, k: (0, 0))
    return pl.pallas_call(
        functools.partial(_mix_kernel, ni),
        grid=(ni + 1, nk),
        in_specs=[pl.BlockSpec((MIX_BM, ka), lambda i, k: (row_block(i), 0)),
                  pl.BlockSpec((MIX_BM, ka), lambda i, k: (row_block(i), 0)),
                  pl.BlockSpec((MIX_BM, MIX_BK), lambda i, k: (row_block(i), col_block(i, k))),
                  pl.BlockSpec((MIX_BM, MIX_BK), lambda i, k: (row_block(i), col_block(i, k) + nk)),
                  pl.BlockSpec((ka, MIX_BK), lambda i, k: (0, col_block(i, k))),
                  pl.BlockSpec((ka, MIX_BK), lambda i, k: (0, col_block(i, k))),
                  pl.BlockSpec((MIX_BK, d), lambda i, k: (col_block(i, k), 0)),
                  pl.BlockSpec((MIX_ROWS, d), slab),
                  vec, mod(2), vec, mod(3), mod(4)],
        out_specs=[pl.BlockSpec((MIX_ROWS, d), slab), pl.BlockSpec((MIX_ROWS, d), slab)],
        out_shape=[jax.ShapeDtypeStruct((t, d), F32), jax.ShapeDtypeStruct((t, d), BF16)],
        scratch_shapes=[pltpu.VMEM((MIX_BM, d), F32), pltpu.VMEM((MIX_BM, d), F32)],
        compiler_params=_params("arbitrary", "arbitrary"),
        name="mix_out",
    )(sgu_act, attn_act, gates, gates, wps, wpa, wout, x2, g_post.reshape(1, d), mod3,
      g_pre.reshape(1, d), mod3, mod3)


FFN_BM = 1024
FFN_BH = 256
EPI_ROWS = 32
EPI_SLABS = FFN_BM // EPI_ROWS


def _ffn_kernel(ni, h_ref, wg_ref, wu_ref, wd_ref, x1_ref, gpost_ref, gt_ref, o_ref,
                acc0_ref, acc1_ref):
    i = pl.program_id(0)
    j = pl.program_id(1)
    accs = (acc0_ref, acc1_ref)
    odd = (i % 2) == 1

    def accumulate(acc_ref):
        h = h_ref[...]
        a = jnp.dot(h, wg_ref[...], preferred_element_type=F32)
        b = jnp.dot(h, wu_ref[...], preferred_element_type=F32)
        g = (jax.nn.silu(a) * b).astype(BF16)
        acc_ref[...] += jnp.dot(g, wd_ref[...], preferred_element_type=F32)

    def write_rows(acc_ref):
        r = pl.multiple_of(jnp.minimum(j, EPI_SLABS - 1) * EPI_ROWS, EPI_ROWS)
        f = acc_ref[pl.ds(r, EPI_ROWS), :]
        o_ref[...] = x1_ref[...] + _rms(f, gt_ref[...] * gpost_ref[...])

    for par in (0, 1):
        is_par = odd if par else jnp.logical_not(odd)

        @pl.when((j == 0) & (i < ni) & is_par)
        def _():
            accs[par][...] = jnp.zeros(accs[par].shape, F32)

    @pl.when(i == 0)
    def _():
        accumulate(accs[0])

    for par in (0, 1):
        is_par = odd if par else jnp.logical_not(odd)

        @pl.when((i > 0) & (i < ni) & is_par)
        def _():
            write_rows(accs[1 - par])
            accumulate(accs[par])

    @pl.when(i == ni)
    def _():
        write_rows(accs[(ni - 1) % 2])


def _ffn(h2, wg, wu, wd, x1, g_post, mod3, seq):
    t, d = h2.shape
    hid = wg.shape[1]
    ni, nj = t // FFN_BM, hid // FFN_BH
    bpb = seq // FFN_BM

    def row_block(i):
        return jnp.minimum(i, ni - 1)

    def hid_block(i, j):
        return jnp.where(i == ni, nj - 1, j)

    def slab(i, j):
        return (jnp.where(i == 0, 0, (i - 1) * EPI_SLABS + jnp.minimum(j, EPI_SLABS - 1)), 0)

    def gate(i, j):
        return ((jnp.maximum(i - 1, 0) // bpb) * N_MOD + 5, 0, 0)

    return pl.pallas_call(
        functools.partial(_ffn_kernel, ni),
        grid=(ni + 1, nj),
        in_specs=[pl.BlockSpec((FFN_BM, d), lambda i, j: (row_block(i), 0),
                               pipeline_mode=pl.Buffered(1)),
                  pl.BlockSpec((d, FFN_BH), lambda i, j: (0, hid_block(i, j))),
                  pl.BlockSpec((d, FFN_BH), lambda i, j: (0, hid_block(i, j))),
                  pl.BlockSpec((FFN_BH, d), lambda i, j: (hid_block(i, j), 0)),
                  pl.BlockSpec((EPI_ROWS, d), slab),
                  pl.BlockSpec((1, d), lambda i, j: (0, 0)),
                  pl.BlockSpec((None, 1, d), gate)],
        out_specs=pl.BlockSpec((EPI_ROWS, d), slab),
        out_shape=jax.ShapeDtypeStruct((t, d), F32),
        scratch_shapes=[pltpu.VMEM((FFN_BM, d), F32), pltpu.VMEM((FFN_BM, d), F32)],
        compiler_params=_params("arbitrary", "arbitrary"),
        name="ffn",
    )(h2, wg, wu, wd, x1, g_post.reshape(1, d), mod3)


def _rope_tables(positions):
    half = ROT_DIM // 2
    inv_freq = ROPE_THETA ** (-jnp.arange(0, ROT_DIM, 2, dtype=F32) / ROT_DIM)
    ang = positions.reshape(-1).astype(F32)[:, None] * inv_freq
    cos, sin = jnp.cos(ang), jnp.sin(ang)
    t = ang.shape[0]
    ones = jnp.ones((t, HEAD_DIM - ROT_DIM), F32)
    zeros_h = jnp.zeros((t, half), F32)
    zeros_r = jnp.zeros((t, HEAD_DIM - ROT_DIM), F32)
    c = jnp.concatenate([cos, cos, ones], axis=1)
    s1 = jnp.concatenate([-sin, zeros_h, zeros_r], axis=1)
    s2 = jnp.concatenate([zeros_h, sin, zeros_r], axis=1)
    return c, s1, s2


def kernel(x, c, positions, w_ada, b_ada, g_pre_mix, w_in, attn_sinks, sgu_ln_g, sgu_ln_b, sgu_w, sgu_b, w_proj_sgu, w_proj_attn, w_out, g_post_mix, g_pre_ffn, w_gate, w_up, w_down, g_post_ffn):
    bsz, seq, d = x.shape
    depth = w_ada.shape[0]
    t = bsz * seq
    xcur = x.reshape(t, d)
    rope_c, rope_s1, rope_s2 = _rope_tables(positions)
    c_pad = jnp.pad(c, ((0, 8 - bsz), (0, 0)))
    for l in range(depth):
        mod = _ada_mod(c_pad, w_ada[l], b_ada[l])[:bsz]
        mod3 = mod.reshape(bsz * N_MOD, 1, d)

        h1 = _prenorm(xcur, g_pre_mix[l], mod3, seq, 0, 1)
        qkv, wps_bf, wpa_bf, wout_bf = _in_proj(
            _rope_epilogue, h1, w_in[l], 0, QKV_WIDTH, tables=(rope_c, rope_s1, rope_s2),
            casts=((w_proj_sgu[l], CAST_ROWS_PROJ), (w_proj_attn[l], CAST_ROWS_PROJ),
                   (w_out[l], CAST_ROWS_OUT)),
            extra_scratch=(pltpu.VMEM((PROJ_BM, PROJ_BN), F32),), name="in_proj_qkv")
        uv, wd_bf = _in_proj(functools.partial(_act_epilogue, _gelu), h1, w_in[l], QKV_WIDTH,
                             2 * SGU_WIDTH, casts=((w_down[l], CAST_ROWS_DOWN),),
                             name="in_proj_uv")
        gates, wg_bf, wu_bf = _in_proj(functools.partial(_act_epilogue, lambda z: z), h1,
                                       w_in[l], QKV_WIDTH + 2 * SGU_WIDTH, 2 * d,
                                       casts=((w_gate[l], CAST_ROWS_UP), (w_up[l], CAST_ROWS_UP)),
                                       name="in_proj_gates")

        attn_act = _attention(qkv, attn_sinks[l], seq)
        sgu_act = _sgu(uv, sgu_ln_g[l], sgu_ln_b[l], sgu_w[l], sgu_b[l])
        x1, h2 = _mix(sgu_act, attn_act, gates, wps_bf, wpa_bf, wout_bf, xcur,
                      g_post_mix[l], g_pre_ffn[l], mod3, seq)

        xcur = _ffn(h2, wg_bf, wu_bf, wd_bf, x1, g_post_ffn[l], mod3, seq)
    return xcur.reshape(bsz, seq, d)
```
